```python
import math
import jax
import jax.numpy as jnp
from jax import lax
import numpy as np

D_MODEL = 4096
BATCH = 4
SEQ = 2048
DEPTH = 4
DEC_BATCH = 32
DEC_SEQ = 1
PAST_LEN = 8192
PAGE_SIZE = 128

N_MIXERS = 3
D_FF = 11008
NORM_EPS = 1e-6
NEG_INF = -1e30
TINY = 1e-30
Q_BLOCK = 128

A_HD = 128
A_HEADS = D_MODEL // (2 * A_HD)
A_KV = 4
A_Q = A_HEADS * 2 * A_HD
A_KW = A_KV * 2 * A_HD

B_HD = 128
B_HEADS = D_MODEL // B_HD
B_KV = 2
B_Q = B_HEADS * B_HD
B_KW = B_KV * B_HD
NSA_BLK = 64
NSA_TOPK = 16
NSA_WIN = 512
NSA_QB = 64
NSA_FORCED = 1e4

C_HD = 64
C_HEADS = D_MODEL // C_HD
C_KV = 8
C_Q = C_HEADS * C_HD
C_KW = C_KV * C_HD
C_WIN = 128

kernel_name = 'hybrid_diff_nsa_swa_macaron_step'


def rmsnorm(x, g):
    xf = x.astype(jnp.float32)
    y = xf * lax.rsqrt(jnp.mean(xf * xf, axis=-1, keepdims=True) + NORM_EPS)
    return (y * g.astype(jnp.float32)).astype(x.dtype)


def swiglu(x, w13, w2):
    gate, up = jnp.split(x @ w13, 2, axis=-1)
    return (jax.nn.silu(gate) * up) @ w2


def alibi_slopes(n_heads, n_groups):
    i = jnp.arange(1, n_heads + 1, dtype=jnp.float32)
    return jnp.exp2(-8.0 * i / n_heads).reshape(n_groups, n_heads // n_groups)


def masked_softmax(s, valid, sink=None):
    s = jnp.where(valid, s, NEG_INF)
    m = jnp.max(s, axis=-1, keepdims=True)
    if sink is not None:
        m = jnp.maximum(m, sink)
    e = jnp.where(valid, jnp.exp(s - m), 0.0)
    den = jnp.sum(e, axis=-1, keepdims=True)
    if sink is not None:
        den = den + jnp.exp(sink - m)
    return e / jnp.maximum(den, TINY)


def gather_pages(cache, page_table):
    g = cache[page_table]
    return g.reshape(page_table.shape[0], -1, *cache.shape[2:])


def sweep_query_blocks(fn, q, q_pos, qb):
    bsz, t = q.shape[:2]
    if t <= qb or t % qb:
        return fn(q, q_pos)
    nb = t // qb
    qs = jnp.moveaxis(q.reshape(bsz, nb, qb, *q.shape[2:]), 1, 0)
    out = lax.map(lambda a: fn(a[0], a[1]), (qs, q_pos.reshape(nb, qb)))
    return jnp.moveaxis(out, 0, 1).reshape(bsz, t, *out.shape[3:])


def local_gqa(q, k, v, q_pos, k_pos, window, slopes, sink=None):
    s = jnp.einsum('bqgrd,bkgd->bgrqk', q, k).astype(jnp.float32) * q.shape[-1] ** -0.5
    dist = q_pos[:, None] - k_pos[None, :]
    valid = (dist >= 0) & (dist <= window) & (k_pos >= 0)[None, :]
    s = s - slopes[:, :, None, None] * dist.astype(jnp.float32)
    p = masked_softmax(s, valid, None if sink is None else sink[:, :, None, None])
    return jnp.einsum('bgrqk,bkgd->bqgrd', p.astype(v.dtype), v)


def banded_prompt_attn(q, k, v, window, slopes, sink=None):
    t = q.shape[1]
    pad = [(0, 0), (window, 0)] + [(0, 0)] * (k.ndim - 2)
    kp, vp = jnp.pad(k, pad), jnp.pad(v, pad)
    span = Q_BLOCK + window

    def blk(qb, pb):
        start = pb[0]
        kb = lax.dynamic_slice_in_dim(kp, start, span, axis=1)
        vb = lax.dynamic_slice_in_dim(vp, start, span, axis=1)
        kpos = start - window + jnp.arange(span, dtype=jnp.int32)
        return local_gqa(qb, kb, vb, pb, kpos, window, slopes, sink)

    return sweep_query_blocks(blk, q, jnp.arange(t, dtype=jnp.int32), Q_BLOCK)


def diff_project(h, w_in):
    b, t, _ = h.shape
    q, k, v = jnp.split(h @ w_in, [A_Q, A_Q + A_KW], axis=-1)
    q = q.reshape(b, t, A_KV, A_HEADS // A_KV, 2, A_HD)
    k = k.reshape(b, t, A_KV, 2, A_HD)
    v = v.reshape(b, t, A_KV, 2 * A_HD)
    return q, k, v


def diff_core(q, k, v, q_pos, k_pos, slopes, lam):
    s = jnp.einsum('bqgrcd,bkgcd->bgrcqk', q, k).astype(jnp.float32) * A_HD ** -0.5
    dist = q_pos[:, None] - k_pos[None, :]
    s = s - slopes[:, :, None, None, None] * dist.astype(jnp.float32)
    p = masked_softmax(s, dist >= 0)
    p = p[:, :, :, 0] - lam * p[:, :, :, 1]
    return jnp.einsum('bgrqk,bkgv->bqgrv', p.astype(v.dtype), v)


def diff_layer(hp, hs, past, cache_k, cache_v, page_table, w_in, lam_vecs, subln_g, w_o, layer_idx):
    lam_init = 0.8 - 0.6 * math.exp(-0.3 * layer_idx)
    lv = lam_vecs.astype(jnp.float32)
    lam = jnp.exp(jnp.sum(lv[0] * lv[1])) - jnp.exp(jnp.sum(lv[2] * lv[3])) + lam_init
    slopes = alibi_slopes(A_HEADS, A_KV)

    def out_proj(o):
        o = rmsnorm(o, subln_g) * (1.0 - lam_init)
        return o.reshape(o.shape[0], o.shape[1], A_Q) @ w_o

    qp, kp, vp = diff_project(hp, w_in)
    pos_p = jnp.arange(hp.shape[1], dtype=jnp.int32)
    op = sweep_query_blocks(lambda qb, pb: diff_core(qb, kp, vp, pb, pos_p, slopes, lam), qp, pos_p, Q_BLOCK)

    qs, ks, vs = diff_project(hs, w_in)
    k_all = jnp.concatenate([gather_pages(cache_k, page_table), ks], axis=1)
    v_all = jnp.concatenate([gather_pages(cache_v, page_table), vs], axis=1)
    tq = hs.shape[1]
    pos_s = past + jnp.arange(tq, dtype=jnp.int32)
    kpos_s = jnp.arange(past + tq, dtype=jnp.int32)
    os_ = sweep_query_blocks(lambda qb, pb: diff_core(qb, k_all, v_all, pb, kpos_s, slopes, lam), qs, pos_s, Q_BLOCK)
    return out_proj(op), out_proj(os_), kp, ks, vp, vs


def nsa_project(h, w_in, b_gate):
    b, t, _ = h.shape
    q, kv, g = jnp.split(h @ w_in, [B_Q, B_Q + 6 * B_KW], axis=-1)
    q = q.reshape(b, t, B_KV, B_HEADS // B_KV, B_HD)
    kv = kv.reshape(b, t, 6, B_KV, B_HD)
    kvs = tuple(kv[:, :, i] for i in range(6))
    g = jax.nn.sigmoid((g + b_gate).astype(jnp.float32)).astype(h.dtype)
    return q, kvs, g.reshape(b, t, B_KV, B_HEADS // B_KV, 3)


def nsa_block(q, pos, kc_cmp, vc_cmp, ks_blk, vs_blk, slopes, n_sel):
    bsz, tq, n_g, n_r, d = q.shape
    nb = kc_cmp.shape[1]
    scale = d ** -0.5
    blk = jnp.arange(nb, dtype=jnp.int32)
    s = jnp.einsum('bqgrd,bngd->bgrqn', q, kc_cmp).astype(jnp.float32) * scale
    dist = pos[:, None] - (blk * NSA_BLK + NSA_BLK - 1)[None, :]
    s = s - slopes[:, :, None, None] * dist.astype(jnp.float32)
    p_cmp = masked_softmax(s, dist >= 0)
    o_cmp = jnp.einsum('bgrqn,bngd->bqgrd', p_cmp.astype(vc_cmp.dtype), vc_cmp)
    cur = pos // NSA_BLK
    forced = (blk[None, :] == 0) | (blk[None, :] == cur[:, None]) | (blk[None, :] == cur[:, None] - 1)
    causal = blk[None, :] * NSA_BLK <= pos[:, None]
    score = jnp.where(forced, NSA_FORCED, jnp.sum(p_cmp, axis=2))
    score = jnp.where(causal, score, -1.0)
    _, idx = lax.top_k(score, n_sel)
    bi = jnp.arange(bsz)[:, None, None, None]
    gi = jnp.arange(n_g)[None, :, None, None]
    kg = ks_blk[bi, gi, idx]
    vg = vs_blk[bi, gi, idx]
    kpos = idx[..., None] * NSA_BLK + jnp.arange(NSA_BLK, dtype=jnp.int32)
    dsel = pos[None, None, :, None, None] - kpos
    m = n_sel * NSA_BLK
    s2 = jnp.einsum('bqgrd,bgqnjd->bgrqnj', q, kg).astype(jnp.float32) * scale
    s2 = s2 - slopes[None, :, :, None, None, None] * dsel[:, :, None].astype(jnp.float32)
    valid2 = (dsel >= 0).reshape(bsz, n_g, 1, tq, m)
    p_sel = masked_softmax(s2.reshape(bsz, n_g, n_r, tq, m), valid2)
    o_sel = jnp.einsum('bgrqm,bgqmd->bqgrd', p_sel.astype(vg.dtype), vg.reshape(bsz, n_g, tq, m, d))
    return jnp.stack([o_cmp, o_sel], axis=-2)


def nsa_cmp_sel(q, q_pos, kc, vc, ks, vs, w_cmp_k, w_cmp_v, slopes):
    bsz, length, n_g, d = kc.shape
    nb = -(-length // NSA_BLK)
    pad = nb * NSA_BLK - length

    def blocks(a):
        return jnp.pad(a, ((0, 0), (0, pad), (0, 0), (0, 0))).reshape(bsz, nb, NSA_BLK, n_g, d)

    kc_cmp = jnp.einsum('bnjgd,jde->bnge', blocks(kc), w_cmp_k)
    vc_cmp = jnp.einsum('bnjgd,jde->bnge', blocks(vc), w_cmp_v)
    ks_blk = jnp.moveaxis(blocks(ks), 3, 1)
    vs_blk = jnp.moveaxis(blocks(vs), 3, 1)
    n_sel = min(NSA_TOPK, nb)
    return sweep_query_blocks(
        lambda qb, pb: nsa_block(qb, pb, kc_cmp, vc_cmp, ks_blk, vs_blk, slopes, n_sel), q, q_pos, NSA_QB)


def nsa_layer(hp, hs, past, cache_kc, cache_vc, cache_ks, cache_vs, buf_kw, buf_vw, page_table,
              w_in, b_gate, w_cmp_k, w_cmp_v, w_o):
    slopes = alibi_slopes(B_HEADS, B_KV)

    def combine(o_cs, o_w, g):
        o = o_cs[..., 0, :] * g[..., 0:1] + o_cs[..., 1, :] * g[..., 1:2] + o_w * g[..., 2:3]
        return o.reshape(o.shape[0], o.shape[1], B_Q) @ w_o

    qp, kvp, gp = nsa_project(hp, w_in, b_gate)
    t = hp.shape[1]
    pos_p = jnp.arange(t, dtype=jnp.int32)
    cs_p = nsa_cmp_sel(qp, pos_p, kvp[0], kvp[1], kvp[2], kvp[3], w_cmp_k, w_cmp_v, slopes)
    win_p = banded_prompt_attn(qp, kvp[4], kvp[5], NSA_WIN, slopes)
    op = combine(cs_p, win_p, gp)

    qs, kvs, gs = nsa_project(hs, w_in, b_gate)
    tq = hs.shape[1]
    pos_s = past + jnp.arange(tq, dtype=jnp.int32)
    kc_all = jnp.concatenate([gather_pages(cache_kc, page_table), kvs[0]], axis=1)
    vc_all = jnp.concatenate([gather_pages(cache_vc, page_table), kvs[1]], axis=1)
    ks_all = jnp.concatenate([gather_pages(cache_ks, page_table), kvs[2]], axis=1)
    vs_all = jnp.concatenate([gather_pages(cache_vs, page_table), kvs[3]], axis=1)
    cs_s = nsa_cmp_sel(qs, pos_s, kc_all, vc_all, ks_all, vs_all, w_cmp_k, w_cmp_v, slopes)
    wb = buf_kw.shape[1]
    kw_all = jnp.concatenate([buf_kw, kvs[4]], axis=1)
    vw_all = jnp.concatenate([buf_vw, kvs[5]], axis=1)
    kpos_w = past - wb + jnp.arange(wb + tq, dtype=jnp.int32)
    win_s = local_gqa(qs, kw_all, vw_all, pos_s, kpos_w, NSA_WIN, slopes)
    os_ = combine(cs_s, win_s, gs)
    wp = min(NSA_WIN, t)
    return (op, os_, kvp[0], kvs[0], kvp[1], kvs[1], kvp[2], kvs[2], kvp[3], kvs[3],
            kvp[4][:, t - wp:], kw_all[:, -wb:], kvp[5][:, t - wp:], vw_all[:, -wb:])


def swa_project(h, w_in, b_in):
    b, t, _ = h.shape
    q, k, v = jnp.split(h @ w_in + b_in, [C_Q, C_Q + C_KW], axis=-1)
    return (q.reshape(b, t, C_KV, C_HEADS // C_KV, C_HD),
            k.reshape(b, t, C_KV, C_HD), v.reshape(b, t, C_KV, C_HD))


def swa_layer(hp, hs, past, buf_k, buf_v, w_in, b_in, sinks, w_o, b_o):
    slopes = alibi_slopes(C_HEADS, C_KV)
    sink = sinks.astype(jnp.float32).reshape(C_KV, C_HEADS // C_KV)

    def out_proj(o):
        return o.reshape(o.shape[0], o.shape[1], C_Q) @ w_o + b_o

    qp, kp, vp = swa_project(hp, w_in, b_in)
    op = banded_prompt_attn(qp, kp, vp, C_WIN, slopes, sink)

    qs, ks, vs = swa_project(hs, w_in, b_in)
    tq = hs.shape[1]
    wb = buf_k.shape[1]
    k_all = jnp.concatenate([buf_k, ks], axis=1)
    v_all = jnp.concatenate([buf_v, vs], axis=1)
    pos_s = past + jnp.arange(tq, dtype=jnp.int32)
    kpos = past - wb + jnp.arange(wb + tq, dtype=jnp.int32)
    os_ = local_gqa(qs, k_all, v_all, pos_s, kpos, C_WIN, slopes, sink)
    t = hp.shape[1]
    wp = min(C_WIN, t)
    return out_proj(op), out_proj(os_), kp[:, t - wp:], k_all[:, -wb:], vp[:, t - wp:], v_all[:, -wb:]


def setup_inputs(seed: int = 0) -> dict:
    key = jax.random.key(seed)
    keys = iter(jax.random.split(key, 48))

    def nrm(shape, scale):
        return jax.random.normal(next(keys), shape, jnp.float32) * scale

    def gain(shape):
        return 1.0 + nrm(shape, 0.01)

    n_pages = PAST_LEN // PAGE_SIZE
    n_used = DEC_BATCH * n_pages
    n_pool = n_used + (n_used + 3) // 4
    page_table = jax.random.permutation(next(keys), n_pool)[:n_used].reshape(DEC_BATCH, n_pages).astype(jnp.int32)
    wb_b = min(NSA_WIN, PAST_LEN)
    wb_c = min(C_WIN, PAST_LEN)
    sd = D_MODEL ** -0.5
    a_k = (n_pool, PAGE_SIZE, A_KV, 2, A_HD)
    a_v = (n_pool, PAGE_SIZE, A_KV, 2 * A_HD)
    b_kv = (n_pool, PAGE_SIZE, B_KV, B_HD)
    a_in = A_Q + 2 * A_KW
    b_in = B_Q + 6 * B_KW + 3 * B_HEADS
    c_in = C_Q + 2 * C_KW
    cmp_sd = (NSA_BLK * B_HD) ** -0.5
    return {
        'x_prompt': nrm((BATCH, SEQ, D_MODEL), 1.0),
        'x_sample': nrm((DEC_BATCH, DEC_SEQ, D_MODEL), 1.0),
        'cache_l0_k': nrm(a_k, 1.0),
        'cache_l0_v': nrm(a_v, 1.0),
        'cache_l1_kc': nrm(b_kv, 1.0),
        'cache_l1_vc': nrm(b_kv, 1.0),
        'cache_l1_ks': nrm(b_kv, 1.0),
        'cache_l1_vs': nrm(b_kv, 1.0),
        'state_l1_kw': nrm((DEC_BATCH, wb_b, B_KV, B_HD), 1.0),
        'state_l1_vw': nrm((DEC_BATCH, wb_b, B_KV, B_HD), 1.0),
        'state_l2_k': nrm((DEC_BATCH, wb_c, C_KV, C_HD), 1.0),
        'state_l2_v': nrm((DEC_BATCH, wb_c, C_KV, C_HD), 1.0),
        'cache_l3_k': nrm(a_k, 1.0),
        'cache_l3_v': nrm(a_v, 1.0),
        'page_table': page_table,
        'norm_ffa': gain((DEPTH, D_MODEL)),
        'norm_mix': gain((DEPTH, D_MODEL)),
        'norm_ffb': gain((DEPTH, D_MODEL)),
        'norm_final': gain((D_MODEL,)),
        'ffa_w13': nrm((DEPTH, D_MODEL, 2 * D_FF), sd),
        'ffa_w2': nrm((DEPTH, D_FF, D_MODEL), D_FF ** -0.5),
        'ffb_w13': nrm((DEPTH, D_MODEL, 2 * D_FF), sd),
        'ffb_w2': nrm((DEPTH, D_FF, D_MODEL), D_FF ** -0.5),
        'l0_w_in': nrm((D_MODEL, a_in), sd),
        'l0_lam': nrm((4, A_HD), 0.1),
        'l0_subln': gain((2 * A_HD,)),
        'l0_w_o': nrm((A_Q, D_MODEL), A_Q ** -0.5),
        'l1_w_in': nrm((D_MODEL, b_in), sd),
        'l1_b_gate': nrm((3 * B_HEADS,), 0.02),
        'l1_w_cmp_k': nrm((NSA_BLK, B_HD, B_HD), cmp_sd),
        'l1_w_cmp_v': nrm((NSA_BLK, B_HD, B_HD), cmp_sd),
        'l1_w_o': nrm((B_Q, D_MODEL), B_Q ** -0.5),
        'l2_w_in': nrm((D_MODEL, c_in), sd),
        'l2_b_in': nrm((c_in,), 0.02),
        'l2_sinks': nrm((C_HEADS,), 0.5),
        'l2_w_o': nrm((C_Q, D_MODEL), C_Q ** -0.5),
        'l2_b_o': nrm((D_MODEL,), 0.02),
        'l3_w_in': nrm((D_MODEL, a_in), sd),
        'l3_lam': nrm((4, A_HD), 0.1),
        'l3_subln': gain((2 * A_HD,)),
        'l3_w_o': nrm((A_Q, D_MODEL), A_Q ** -0.5),
    }


def reference(x_prompt, x_sample, cache_l0_k, cache_l0_v, cache_l1_kc, cache_l1_vc, cache_l1_ks, cache_l1_vs,
              state_l1_kw, state_l1_vw, state_l2_k, state_l2_v, cache_l3_k, cache_l3_v, page_table,
              norm_ffa, norm_mix, norm_ffb, norm_final, ffa_w13, ffa_w2, ffb_w13, ffb_w2,
              l0_w_in, l0_lam, l0_subln, l0_w_o,
              l1_w_in, l1_b_gate, l1_w_cmp_k, l1_w_cmp_v, l1_w_o,
              l2_w_in, l2_b_in, l2_sinks, l2_w_o, l2_b_o,
              l3_w_in, l3_lam, l3_subln, l3_w_o):
    past = page_table.shape[1] * PAGE_SIZE
    a_params = {0: (cache_l0_k, cache_l0_v, l0_w_in, l0_lam, l0_subln, l0_w_o),
                3: (cache_l3_k, cache_l3_v, l3_w_in, l3_lam, l3_subln, l3_w_o)}
    b_params = {1: (cache_l1_kc, cache_l1_vc, cache_l1_ks, cache_l1_vs, state_l1_kw, state_l1_vw,
                    l1_w_in, l1_b_gate, l1_w_cmp_k, l1_w_cmp_v, l1_w_o)}
    c_params = {2: (state_l2_k, state_l2_v, l2_w_in, l2_b_in, l2_sinks, l2_w_o, l2_b_o)}
    new_state = {}
    xp, xs = x_prompt, x_sample
    for i in range(DEPTH):
        xp = xp + 0.5 * swiglu(rmsnorm(xp, norm_ffa[i]), ffa_w13[i], ffa_w2[i])
        xs = xs + 0.5 * swiglu(rmsnorm(xs, norm_ffa[i]), ffa_w13[i], ffa_w2[i])
        hp = rmsnorm(xp, norm_mix[i])
        hs = rmsnorm(xs, norm_mix[i])
        kind = i % N_MIXERS
        if kind == 0:
            ck, cv, w_in, lam, subln, w_o = a_params[i]
            op, os_, *st = diff_layer(hp, hs, past, ck, cv, page_table, w_in, lam, subln, w_o, i)
        elif kind == 1:
            ckc, cvc, cks, cvs, bkw, bvw, w_in, b_gate, w_ck, w_cv, w_o = b_params[i]
            op, os_, *st = nsa_layer(hp, hs, past, ckc, cvc, cks, cvs, bkw, bvw, page_table,
                                     w_in, b_gate, w_ck, w_cv, w_o)
        else:
            bk, bv, w_in, b_in, sinks, w_o, b_o = c_params[i]
            op, os_, *st = swa_layer(hp, hs, past, bk, bv, w_in, b_in, sinks, w_o, b_o)
        new_state[i] = st
        xp = xp + op
        xs = xs + os_
        xp = xp + 0.5 * swiglu(rmsnorm(xp, norm_ffb[i]), ffb_w13[i], ffb_w2[i])
        xs = xs + 0.5 * swiglu(rmsnorm(xs, norm_ffb[i]), ffb_w13[i], ffb_w2[i])
    y_prompt = rmsnorm(xp, norm_final)
    y_sample = rmsnorm(xs, norm_final)
    l0_kp, l0_ksm, l0_vp, l0_vsm = new_state[0]
    (l1_kcp, l1_kcsm, l1_vcp, l1_vcsm, l1_ksp, l1_kssm, l1_vsp, l1_vssm,
     l1_kwp, l1_kwsm, l1_vwp, l1_vwsm) = new_state[1]
    l2_kp, l2_ksm, l2_vp, l2_vsm = new_state[2]
    l3_kp, l3_ksm, l3_vp, l3_vsm = new_state[3]
    return (y_prompt, y_sample,
            l0_kp, l0_ksm, l0_vp, l0_vsm,
            l1_kcp, l1_kcsm, l1_vcp, l1_vcsm, l1_ksp, l1_kssm, l1_vsp, l1_vssm,
            l1_kwp, l1_kwsm, l1_vwp, l1_vwsm,
            l2_kp, l2_ksm, l2_vp, l2_vsm,
            l3_kp, l3_ksm, l3_vp, l3_vsm)
```

```python
import functools
import math

import jax
import jax.numpy as jnp
from jax import lax
from jax.experimental import pallas as pl
from jax.experimental.pallas import tpu as pltpu

NORM_EPS = 1e-6
NEG_INF = -1e30
TINY = 1e-30

D_MODEL = 4096
PAGE = 128
A_HD, A_HEADS, A_KV = 128, 16, 4
B_HD, B_HEADS, B_KV = 128, 32, 2
NSA_BLK, NSA_TOPK, NSA_WIN, NSA_FORCED = 64, 16, 512, 1e4
C_HD, C_HEADS, C_KV, C_WIN = 64, 64, 8, 128

LANE = 128
QT = 128
MIB = 1024 * 1024
BF16 = jnp.bfloat16
F32 = jnp.float32


def _cparams(sem, vmem_mib):
    return pltpu.CompilerParams(dimension_semantics=sem, vmem_limit_bytes=vmem_mib * MIB)


def _round_up(x, m):
    return -(-x // m) * m


def _pick(n, prefs):
    for p in prefs:
        if n % p == 0:
            return p
    return n


def _rms_kernel(x_ref, g_ref, o_ref):
    x = x_ref[...]
    ms = jnp.mean(x * x, axis=-1, keepdims=True)
    o_ref[...] = (x * lax.rsqrt(ms + NORM_EPS) * g_ref[...]).astype(o_ref.dtype)


def rmsnorm(x, g, out_dtype):
    m, d = x.shape
    tm = _pick(m, (256, 128, 64, 32, 16, 8))
    return pl.pallas_call(
        _rms_kernel,
        grid=(m // tm,),
        in_specs=[pl.BlockSpec((tm, d), lambda i: (i, 0)), pl.BlockSpec((1, d), lambda i: (0, 0))],
        out_specs=pl.BlockSpec((tm, d), lambda i: (i, 0)),
        out_shape=jax.ShapeDtypeStruct((m, d), out_dtype),
        compiler_params=_cparams(("parallel",), 40),
        name="rmsnorm",
    )(x, g.reshape(1, d).astype(F32))


def _mm_kernel(*refs, nk, has_bias, has_res, res_scale):
    it = iter(refs)
    a_ref, w_ref = next(it), next(it)
    b_ref = next(it) if has_bias else None
    r_ref = next(it) if has_res else None
    o_ref = next(it)
    acc_ref = next(it) if nk > 1 else None

    def epilogue(acc):
        if has_bias:
            acc = acc + b_ref[...]
        if has_res:
            acc = r_ref[...] + res_scale * acc
        o_ref[...] = acc.astype(o_ref.dtype)

    d = jnp.dot(a_ref[...], w_ref[...], preferred_element_type=F32)
    if nk == 1:
        epilogue(d)
    else:
        k = pl.program_id(2)

        @pl.when(k == 0)
        def _():
            acc_ref[...] = d

        @pl.when(k > 0)
        def _():
            acc_ref[...] += d

        @pl.when(k == nk - 1)
        def _():
            epilogue(acc_ref[...])


def matmul(a, w, *, bias=None, res=None, res_scale=1.0, out_dtype=F32):
    m, k = a.shape
    _, n = w.shape
    tm = _pick(m, (1024, 512, 256, 128, 64, 32, 16, 8))
    tn = _pick(n, (512, 256, 128))
    tk = k if k <= 4096 else _pick(k, (2816, 2048, 1408, 1024, 512, 256, 128))
    nk = k // tk
    in_specs = [pl.BlockSpec((tm, tk), lambda i, j, kk: (i, kk)),
                pl.BlockSpec((tk, tn), lambda i, j, kk: (kk, j))]
    args = [a, w]
    if bias is not None:
        in_specs.append(pl.BlockSpec((1, tn), lambda i, j, kk: (0, j)))
        args.append(bias.reshape(1, n).astype(F32))
    if res is not None:
        in_specs.append(pl.BlockSpec((tm, tn), lambda i, j, kk: (i, j)))
        args.append(res)
    return pl.pallas_call(
        functools.partial(_mm_kernel, nk=nk, has_bias=bias is not None, has_res=res is not None,
                          res_scale=res_scale),
        grid=(m // tm, n // tn, nk),
        in_specs=in_specs,
        out_specs=pl.BlockSpec((tm, tn), lambda i, j, kk: (i, j)),
        out_shape=jax.ShapeDtypeStruct((m, n), out_dtype),
        scratch_shapes=[pltpu.VMEM((tm, tn), F32)] if nk > 1 else [],
        compiler_params=_cparams(("parallel", "parallel", "arbitrary"), 56),
        name="matmul",
    )(*args)


def _ffn1_kernel(a_ref, wg_ref, wu_ref, o_ref):
    a = a_ref[...]
    g = jnp.dot(a, wg_ref[...], preferred_element_type=F32)
    u = jnp.dot(a, wu_ref[...], preferred_element_type=F32)
    o_ref[...] = (g / (1.0 + jnp.exp(-g)) * u).astype(o_ref.dtype)


def ffn1(a, wg, wu):
    m, k = a.shape
    _, n = wg.shape
    tm = _pick(m, (1024, 512, 256, 128, 64, 32, 16, 8))
    tn = _pick(n, (512, 256, 128))
    return pl.pallas_call(
        _ffn1_kernel,
        grid=(m // tm, n // tn),
        in_specs=[pl.BlockSpec((tm, k), lambda i, j: (i, 0)),
                  pl.BlockSpec((k, tn), lambda i, j: (0, j)),
                  pl.BlockSpec((k, tn), lambda i, j: (0, j))],
        out_specs=pl.BlockSpec((tm, tn), lambda i, j: (i, j)),
        out_shape=jax.ShapeDtypeStruct((m, n), BF16),
        compiler_params=_cparams(("parallel", "parallel"), 56),
        name="ffn1",
    )(a, wg, wu)


def _pad_cols(w, n_to):
    return jnp.pad(w, ((0, 0), (0, n_to - w.shape[1])))


def ffn_block(xp, xs, g, w13, w2):
    f = w2.shape[0]
    fp = _round_up(f, 512)
    wg = _pad_cols(w13[:, :f], fp).astype(BF16)
    wu = _pad_cols(w13[:, f:], fp).astype(BF16)
    w2b = jnp.pad(w2, ((0, fp - f), (0, 0))).astype(BF16)
    outs = []
    for x in (xp, xs):
        h = rmsnorm(x, g, BF16)
        mid = ffn1(h, wg, wu)
        outs.append(matmul(mid, w2b, res=x, res_scale=0.5))
    return outs


def _flash_kernel(*refs, n_maps, n_rep, dv, nkt, window, scale, has_sel, has_sink, diff, lam_init,
                  head_major_out):
    it = iter(refs)
    q_ref, k_ref, v_ref, slope_ref = next(it), next(it), next(it), next(it)
    sink_ref = next(it) if has_sink else None
    sel_ref, e_ref = (next(it), next(it)) if has_sel else (None, None)
    lam_ref, subln_ref = (next(it), next(it)) if diff else (None, None)
    o_ref = next(it)
    m_ref, l_ref, acc_ref = next(it), next(it), next(it)

    t = QT
    rows = n_rep * t
    i = pl.program_id(2)
    kt = pl.program_id(3)
    if window is None:
        j = kt
        active = kt <= i
    else:
        j = i - (nkt - 1) + kt
        active = j >= 0

    @pl.when(kt == 0)
    def _():
        if has_sink:
            m_ref[...] = jnp.broadcast_to(sink_ref[...][None], m_ref.shape)
            l_ref[...] = jnp.ones(l_ref.shape, F32)
        else:
            m_ref[...] = jnp.full(m_ref.shape, NEG_INF, F32)
            l_ref[...] = jnp.zeros(l_ref.shape, F32)
        acc_ref[...] = jnp.zeros(acc_ref.shape, F32)

    @pl.when(active)
    def _():
        dist = (i - j) * t + (lax.broadcasted_iota(jnp.int32, (t, t), 0)
                              - lax.broadcasted_iota(jnp.int32, (t, t), 1))
        ok = dist >= 0
        if window is not None:
            ok = ok & (dist <= window)
        okf = jnp.where(ok, 1.0, 0.0)
        if has_sel:
            okf = okf * jnp.dot(sel_ref[...].astype(BF16), e_ref[...], preferred_element_type=F32)
        distf = jnp.concatenate([dist.astype(F32)] * n_rep, axis=0)
        valid = jnp.concatenate([okf] * n_rep, axis=0) > 0.5
        slope = slope_ref[...]
        v = v_ref[...]
        for c in range(n_maps):
            q = q_ref[c].reshape(rows, q_ref.shape[-1])
            s = lax.dot_general(q, k_ref[c], (((1,), (1,)), ((), ())), preferred_element_type=F32)
            s = s * scale - slope * distf
            s = jnp.where(valid, s, NEG_INF)
            m_old = m_ref[c]
            m_new = jnp.maximum(m_old, jnp.max(s, axis=-1, keepdims=True))
            alpha = jnp.exp(m_old - m_new)
            e = jnp.where(valid, jnp.exp(s - m_new), 0.0)
            l_ref[c] = alpha * l_ref[c] + jnp.sum(e, axis=-1, keepdims=True)
            acc_ref[c] = alpha * acc_ref[c] + jnp.dot(e.astype(BF16), v, preferred_element_type=F32)
            m_ref[c] = m_new

    @pl.when(kt == nkt - 1)
    def _():
        o = acc_ref[0] / jnp.maximum(l_ref[0], TINY)
        if diff:
            lv = lam_ref[...]
            lam = (jnp.exp(jnp.sum(lv[0:1] * lv[1:2], axis=-1, keepdims=True))
                   - jnp.exp(jnp.sum(lv[2:3] * lv[3:4], axis=-1, keepdims=True)) + lam_init)
            o = o - lam * (acc_ref[1] / jnp.maximum(l_ref[1], TINY))
            ms = jnp.mean(o * o, axis=-1, keepdims=True)
            o = o * lax.rsqrt(ms + NORM_EPS) * subln_ref[...] * (1.0 - lam_init)
        if head_major_out:
            o_ref[...] = o.reshape(n_rep, t, dv).astype(o_ref.dtype)
        else:
            for r in range(n_rep):
                o_ref[:, r * dv:(r + 1) * dv] = o[r * t:(r + 1) * t].astype(o_ref.dtype)


def flash_prompt(q, k, v, slopes, *, window=None, sink=None, sel=None, diff=None, out_dtype=F32,
                 head_major_out=False):
    b, g, n_maps, n_rep, tlen, d = q.shape
    dv = v.shape[-1]
    t = QT
    nq = tlen // t
    rows = n_rep * t
    nkt = nq if window is None else window // t + 1
    if window is None:
        kidx = lambda i, kt: jnp.minimum(kt, i)
    else:
        kidx = lambda i, kt: jnp.maximum(i - (nkt - 1) + kt, 0)
    col = lambda x: jnp.repeat(x.astype(F32).reshape(g, n_rep, 1), t, axis=1).reshape(g, rows, 1)
    in_specs = [
        pl.BlockSpec((None, None, n_maps, n_rep, t, d), lambda bb, gg, i, kt: (bb, gg, 0, 0, i, 0)),
        pl.BlockSpec((None, None, n_maps, t, d), lambda bb, gg, i, kt: (bb, gg, 0, kidx(i, kt), 0)),
        pl.BlockSpec((None, None, t, dv), lambda bb, gg, i, kt: (bb, gg, kidx(i, kt), 0)),
        pl.BlockSpec((None, rows, 1), lambda bb, gg, i, kt: (gg, 0, 0)),
    ]
    args = [q, k, v, col(slopes)]
    if sink is not None:
        in_specs.append(pl.BlockSpec((None, rows, 1), lambda bb, gg, i, kt: (gg, 0, 0)))
        args.append(col(sink))
    if sel is not None:
        sel_mask, expand = sel
        nb = sel_mask.shape[-1]
        in_specs.append(pl.BlockSpec((None, None, t, nb), lambda bb, gg, i, kt: (bb, gg, i, 0)))
        in_specs.append(pl.BlockSpec((nb, t), lambda bb, gg, i, kt: (0, kidx(i, kt))))
        args += [sel_mask, expand]
    lam_init = 0.0
    if diff is not None:
        lam_vecs, subln, lam_init = diff
        in_specs.append(pl.BlockSpec((4, A_HD), lambda bb, gg, i, kt: (0, 0)))
        in_specs.append(pl.BlockSpec((1, dv), lambda bb, gg, i, kt: (0, 0)))
        args += [lam_vecs.astype(F32), subln.reshape(1, dv).astype(F32)]
    if head_major_out:
        out_shape = jax.ShapeDtypeStruct((b, g, n_rep, tlen, dv), out_dtype)
        out_spec = pl.BlockSpec((None, None, n_rep, t, dv), lambda bb, gg, i, kt: (bb, gg, 0, i, 0))
    else:
        out_shape = jax.ShapeDtypeStruct((b, tlen, g * n_rep * dv), out_dtype)
        out_spec = pl.BlockSpec((None, t, n_rep * dv), lambda bb, gg, i, kt: (bb, i, gg))
    return pl.pallas_call(
        functools.partial(_flash_kernel, n_maps=n_maps, n_rep=n_rep, dv=dv, nkt=nkt, window=window,
                          scale=d ** -0.5, has_sel=sel is not None, has_sink=sink is not None,
                          diff=diff is not None, lam_init=lam_init, head_major_out=head_major_out),
        grid=(b, g, nq, nkt),
        in_specs=in_specs,
        out_specs=out_spec,
        out_shape=out_shape,
        scratch_shapes=[pltpu.VMEM((n_maps, rows, 1), F32), pltpu.VMEM((n_maps, rows, 1), F32),
                        pltpu.VMEM((n_maps, rows, dv), F32)],
        compiler_params=_cparams(("parallel", "parallel", "parallel", "arbitrary"), 48),
        name="flash_prompt",
    )(*args)


def alibi_slopes(n_heads, n_groups):
    i = jnp.arange(1, n_heads + 1, dtype=F32)
    return jnp.exp2(-8.0 * i / n_heads).reshape(n_groups, n_heads // n_groups)


def _rank_select(score, block_id, n_sel, ids):
    rank = jnp.zeros(score.shape, F32)
    for m, bid in enumerate(ids):
        cm = score[:, m:m + 1]
        beats = (cm > score) | ((cm == score) & (bid < block_id))
        rank = rank + jnp.where(beats, 1.0, 0.0)
    return jnp.where(rank < n_sel, 1.0, 0.0)


def _nsa_cmp_prompt_kernel(q_ref, kc_ref, vc_ref, slope_ref, o_ref, sel_ref, *, n_rep, nb, n_sel, scale):
    t = QT
    rows = n_rep * t
    nbp = kc_ref.shape[0]
    i = pl.program_id(2)
    q = q_ref[...].reshape(rows, q_ref.shape[-1])
    s = lax.dot_general(q, kc_ref[...], (((1,), (1,)), ((), ())), preferred_element_type=F32)
    pos = i * t + lax.broadcasted_iota(jnp.int32, (t, nbp), 0)
    blk = lax.broadcasted_iota(jnp.int32, (t, nbp), 1)
    real = blk < nb
    dist = pos - (blk * NSA_BLK + NSA_BLK - 1)
    distf = jnp.concatenate([jnp.where(real, dist, 0).astype(F32)] * n_rep, axis=0)
    valid = jnp.concatenate([jnp.where((dist >= 0) & real, 1.0, 0.0)] * n_rep, axis=0) > 0.5
    s = s * scale - slope_ref[...] * distf
    s = jnp.where(valid, s, NEG_INF)
    m = jnp.max(s, axis=-1, keepdims=True)
    e = jnp.where(valid, jnp.exp(s - m), 0.0)
    p = e / jnp.maximum(jnp.sum(e, axis=-1, keepdims=True), TINY)
    o = jnp.dot(p.astype(BF16), vc_ref[...], preferred_element_type=F32)
    d = o.shape[-1]
    for r in range(n_rep):
        o_ref[:, r * d:(r + 1) * d] = o[r * t:(r + 1) * t]
    score = p[0:t]
    for r in range(1, n_rep):
        score = score + p[r * t:(r + 1) * t]
    cur = pos // NSA_BLK
    forced = (blk == 0) | (blk == cur) | (blk == cur - 1)
    causal = blk * NSA_BLK <= pos
    score = jnp.where(forced, NSA_FORCED, score)
    score = jnp.where(causal, score, -1.0)
    score = jnp.where(real, score, -2.0)
    sel_ref[...] = _rank_select(score, blk, n_sel, range(nb))


def nsa_cmp_prompt(q, kc_cmp, vc_cmp, slopes):
    b, g, n_rep, tlen, d = q.shape
    nb = kc_cmp.shape[2]
    nbp = _round_up(nb, LANE)
    kc_cmp = jnp.pad(kc_cmp, ((0, 0), (0, 0), (0, nbp - nb), (0, 0)))
    vc_cmp = jnp.pad(vc_cmp, ((0, 0), (0, 0), (0, nbp - nb), (0, 0)))
    t = QT
    rows = n_rep * t
    slope_col = jnp.repeat(slopes.reshape(g, n_rep, 1), t, axis=1).reshape(g, rows, 1)
    return pl.pallas_call(
        functools.partial(_nsa_cmp_prompt_kernel, n_rep=n_rep, nb=nb, n_sel=min(NSA_TOPK, nb), scale=d ** -0.5),
        grid=(b, g, tlen // t),
        in_specs=[pl.BlockSpec((None, None, n_rep, t, d), lambda bb, gg, i: (bb, gg, 0, i, 0)),
                  pl.BlockSpec((None, None, nbp, d), lambda bb, gg, i: (bb, gg, 0, 0)),
                  pl.BlockSpec((None, None, nbp, d), lambda bb, gg, i: (bb, gg, 0, 0)),
                  pl.BlockSpec((None, rows, 1), lambda bb, gg, i: (gg, 0, 0))],
        out_specs=[pl.BlockSpec((None, t, n_rep * d), lambda bb, gg, i: (bb, i, gg)),
                   pl.BlockSpec((None, None, t, nbp), lambda bb, gg, i: (bb, gg, i, 0))],
        out_shape=[jax.ShapeDtypeStruct((b, tlen, g * n_rep * d), F32),
                   jax.ShapeDtypeStruct((b, g, tlen, nbp), F32)],
        compiler_params=_cparams(("parallel", "parallel", "parallel"), 48),
        name="nsa_cmp_prompt",
    )(q, kc_cmp, vc_cmp, slope_col)


def _nsa_combine_kernel(oc_ref, os_ref, ow_ref, gl_ref, bg_ref, o_ref, *, n_heads, d):
    gates = 1.0 / (1.0 + jnp.exp(-(gl_ref[...] + bg_ref[...])))
    for h in range(n_heads):
        sl = slice(h * d, (h + 1) * d)
        o = (oc_ref[:, sl] * gates[:, 3 * h:3 * h + 1] + os_ref[:, sl] * gates[:, 3 * h + 1:3 * h + 2]
             + ow_ref[:, sl] * gates[:, 3 * h + 2:3 * h + 3])
        o_ref[:, sl] = o.astype(o_ref.dtype)


def nsa_combine(o_cmp, o_sel, o_win, proj, b_gate, gate_col):
    m, n = o_cmp.shape
    tm = _pick(m, (256, 128, 64, 32, 16, 8))
    blk = lambda: pl.BlockSpec((tm, n), lambda i: (i, 0))
    bg = jnp.pad(b_gate.astype(F32), (0, LANE - b_gate.shape[0])).reshape(1, LANE)
    return pl.pallas_call(
        functools.partial(_nsa_combine_kernel, n_heads=B_HEADS, d=B_HD),
        grid=(m // tm,),
        in_specs=[blk(), blk(), blk(),
                  pl.BlockSpec((tm, LANE), lambda i: (i, gate_col // LANE)),
                  pl.BlockSpec((1, LANE), lambda i: (0, 0))],
        out_specs=blk(),
        out_shape=jax.ShapeDtypeStruct((m, n), BF16),
        compiler_params=_cparams(("parallel",), 48),
        name="nsa_combine",
    )(o_cmp, o_sel, o_win, proj, bg)


def _decode_kernel(*refs, n_grp, n_maps, n_rep, dv, n_pages, n_steps, paged, chunk, kpos0, pos, window,
                   scale, has_mask, has_sink, diff, lam_init):
    it = iter(refs)
    if paged:
        next(it)
    q_ref = next(it)
    k_refs = [next(it) for _ in range(n_pages)]
    v_refs = [next(it) for _ in range(n_pages)]
    kn_ref, vn_ref, slope_ref = next(it), next(it), next(it)
    sink_ref = next(it) if has_sink else None
    mask_ref, maskn_ref = (next(it), next(it)) if has_mask else (None, None)
    lam_ref, subln_ref = (next(it), next(it)) if diff else (None, None)
    o_ref = next(it)
    m_ref, l_ref, acc_ref = next(it), next(it), next(it)

    step = pl.program_id(1)

    @pl.when(step == 0)
    def _():
        if has_sink:
            for g in range(n_grp):
                m_ref[g] = jnp.broadcast_to(sink_ref[g][None], m_ref.shape[1:])
            l_ref[...] = jnp.ones(l_ref.shape, F32)
        else:
            m_ref[...] = jnp.full(m_ref.shape, NEG_INF, F32)
            l_ref[...] = jnp.zeros(l_ref.shape, F32)
        acc_ref[...] = jnp.zeros(acc_ref.shape, F32)

    kpos = kpos0 + step * chunk + lax.broadcasted_iota(jnp.int32, (1, chunk), 1)
    dist = pos - kpos
    ok = (dist >= 0) & (kpos >= 0)
    if window is not None:
        ok = ok & (dist <= window)
    distf = dist.astype(F32)
    for g in range(n_grp):
        okf = jnp.where(ok, 1.0, 0.0)
        if has_mask:
            okf = okf * mask_ref[g]
        okg = jnp.broadcast_to(okf, (n_rep, chunk)) > 0.5
        vcat = v_refs[0][g] if n_pages == 1 else jnp.concatenate([r[g] for r in v_refs], axis=0)
        slope = slope_ref[g]
        for c in range(n_maps):
            gc = g * n_maps + c
            kcat = k_refs[0][gc] if n_pages == 1 else jnp.concatenate([r[gc] for r in k_refs], axis=0)
            s = lax.dot_general(q_ref[g, c], kcat, (((1,), (1,)), ((), ())), preferred_element_type=F32)
            s = s * scale - slope * distf
            s = jnp.where(okg, s, NEG_INF)
            m_old = m_ref[g, c]
            m_new = jnp.maximum(m_old, jnp.max(s, axis=-1, keepdims=True))
            alpha = jnp.exp(m_old - m_new)
            e = jnp.where(okg, jnp.exp(s - m_new), 0.0)
            l_ref[g, c] = alpha * l_ref[g, c] + jnp.sum(e, axis=-1, keepdims=True)
            acc_ref[g, c] = alpha * acc_ref[g, c] + jnp.dot(e.astype(BF16), vcat, preferred_element_type=F32)
            m_ref[g, c] = m_new

    @pl.when(step == n_steps - 1)
    def _():
        if diff:
            lv = lam_ref[...]
            lam = (jnp.exp(jnp.sum(lv[0:1] * lv[1:2], axis=-1, keepdims=True))
                   - jnp.exp(jnp.sum(lv[2:3] * lv[3:4], axis=-1, keepdims=True)) + lam_init)
        for g in range(n_grp):
            outs = []
            for c in range(n_maps):
                gc = g * n_maps + c
                s = jnp.sum(q_ref[g, c].astype(F32) * kn_ref[gc].astype(F32), axis=-1, keepdims=True) * scale
                if has_mask:
                    okn = maskn_ref[g][:, 0:1] > 0.5
                    s = jnp.where(okn, s, NEG_INF)
                m_old = m_ref[g, c]
                m_new = jnp.maximum(m_old, s)
                alpha = jnp.exp(m_old - m_new)
                e = jnp.exp(s - m_new)
                if has_mask:
                    e = jnp.where(okn, e, 0.0)
                den = alpha * l_ref[g, c] + e
                acc = alpha * acc_ref[g, c] + e * vn_ref[g].astype(F32)
                outs.append(acc / jnp.maximum(den, TINY))
            o = outs[0]
            if diff:
                o = o - lam * outs[1]
                ms = jnp.mean(o * o, axis=-1, keepdims=True)
                o = o * lax.rsqrt(ms + NORM_EPS) * subln_ref[...] * (1.0 - lam_init)
            o_ref[g] = o.astype(o_ref.dtype)


def decode_attn(q, k_main, v_main, k_new, v_new, slopes, *, pos, page_table=None, pages_per_step=1,
                window=None, sink=None, mask=None, diff=None, out_dtype=F32):
    b, n_grp, n_maps, n_rep, d = q.shape
    dv = v_new.shape[-1]
    paged = page_table is not None
    if paged:
        n_pages = pages_per_step
        n_steps = page_table.shape[1] // n_pages
        chunk = n_pages * PAGE
        kpos0 = 0
        total = page_table.shape[1] * PAGE
    else:
        n_pages, n_steps = 1, 1
        chunk = k_main.shape[2]
        kpos0 = pos - chunk
        total = chunk

    def idx(f):
        if paged:
            return lambda bb, st, pt: f(bb, st, pt)
        return lambda bb, st: f(bb, st, None)

    in_specs = [pl.BlockSpec((None, n_grp, n_maps, n_rep, d), idx(lambda bb, st, pt: (bb, 0, 0, 0, 0)))]
    args = [q]
    for main, width, heads in ((k_main, d, n_grp * n_maps), (v_main, dv, n_grp)):
        for p in range(n_pages):
            if paged:
                in_specs.append(pl.BlockSpec(
                    (None, heads, PAGE, width),
                    lambda bb, st, pt, p=p: (pt[bb, st * n_pages + p], 0, 0, 0)))
            else:
                in_specs.append(pl.BlockSpec((None, heads, chunk, width), lambda bb, st: (bb, 0, 0, 0)))
            args.append(main)
    in_specs += [pl.BlockSpec((None, n_grp * n_maps, 1, d), idx(lambda bb, st, pt: (bb, 0, 0, 0))),
                 pl.BlockSpec((None, n_grp, 1, dv), idx(lambda bb, st, pt: (bb, 0, 0, 0))),
                 pl.BlockSpec((n_grp, n_rep, 1), idx(lambda bb, st, pt: (0, 0, 0)))]
    args += [k_new, v_new, slopes.astype(F32).reshape(n_grp, n_rep, 1)]
    if sink is not None:
        in_specs.append(pl.BlockSpec((n_grp, n_rep, 1), idx(lambda bb, st, pt: (0, 0, 0))))
        args.append(sink.astype(F32).reshape(n_grp, n_rep, 1))
    if mask is not None:
        in_specs.append(pl.BlockSpec((None, n_grp, 1, chunk), idx(lambda bb, st, pt: (bb, 0, 0, st))))
        in_specs.append(pl.BlockSpec((None, n_grp, 1, LANE), idx(lambda bb, st, pt: (bb, 0, 0, total // LANE))))
        args += [mask, mask]
    lam_init = 0.0
    if diff is not None:
        lam_vecs, subln, lam_init = diff
        in_specs.append(pl.BlockSpec((4, A_HD), idx(lambda bb, st, pt: (0, 0))))
        in_specs.append(pl.BlockSpec((1, dv), idx(lambda bb, st, pt: (0, 0))))
        args += [lam_vecs.astype(F32), subln.reshape(1, dv).astype(F32)]
    out_spec = pl.BlockSpec((None, n_grp, n_rep, dv), idx(lambda bb, st, pt: (bb, 0, 0, 0)))
    scratch = [pltpu.VMEM((n_grp, n_maps, n_rep, 1), F32), pltpu.VMEM((n_grp, n_maps, n_rep, 1), F32),
               pltpu.VMEM((n_grp, n_maps, n_rep, dv), F32)]
    body = functools.partial(
        _decode_kernel, n_grp=n_grp, n_maps=n_maps, n_rep=n_rep, dv=dv, n_pages=n_pages, n_steps=n_steps,
        paged=paged, chunk=chunk, kpos0=kpos0, pos=pos, window=window, scale=d ** -0.5,
        has_mask=mask is not None, has_sink=sink is not None, diff=diff is not None, lam_init=lam_init)
    out_shape = jax.ShapeDtypeStruct((b, n_grp, n_rep, dv), out_dtype)
    cp = _cparams(("parallel", "arbitrary"), 48)
    if paged:
        return pl.pallas_call(
            body,
            grid_spec=pltpu.PrefetchScalarGridSpec(num_scalar_prefetch=1, grid=(b, n_steps), in_specs=in_specs,
                                                   out_specs=out_spec, scratch_shapes=scratch),
            out_shape=out_shape, compiler_params=cp, name="decode_paged",
        )(page_table, *args)
    return pl.pallas_call(body, grid=(b, n_steps), in_specs=in_specs, out_specs=out_spec, out_shape=out_shape,
                          scratch_shapes=scratch, compiler_params=cp, name="decode_dense")(*args)


def _nsa_cmp_sample_kernel(pt_ref, q_ref, kpool_ref, vpool_ref, kn_ref, vn_ref, wk0_ref, wv0_ref, slope_ref,
                           e_ref, o_ref, mask_ref, kc_scr, vc_scr, *, n_grp, n_rep, n_pg, nbp, n_sel, pos,
                           scale, ids):
    b = pl.program_id(0)
    d = q_ref.shape[-1]
    gw = n_grp * d
    for p in range(n_pg):
        pid = pt_ref[b, p]
        kc_scr[pl.ds(p, 1), :] = kpool_ref[pl.ds(pid, 1), :]
        vc_scr[pl.ds(p, 1), :] = vpool_ref[pl.ds(pid, 1), :]
    nb = 2 * n_pg + 1
    col = lax.broadcasted_iota(jnp.int32, (1, nbp), 1)
    nid = jnp.where(col < n_pg, 2 * col, jnp.where(col < 2 * n_pg, 2 * (col - n_pg) + 1,
                                                   jnp.where(col == 2 * n_pg, 2 * n_pg, 1 << 20)))
    real = col < nb
    dist = pos - (nid * NSA_BLK + NSA_BLK - 1)
    valid = (dist >= 0) & real
    distf = jnp.where(real, dist, 0).astype(F32)
    cur = pos // NSA_BLK
    forced = (nid == 0) | (nid == cur) | (nid == cur - 1)
    causal = nid * NSA_BLK <= pos
    kn_cmp = jnp.dot(kn_ref[...].astype(BF16), wk0_ref[...], preferred_element_type=F32)
    vn_cmp = jnp.dot(vn_ref[...].astype(BF16), wv0_ref[...], preferred_element_type=F32)
    tail = nbp - 2 * n_pg
    first_row = lax.broadcasted_iota(jnp.int32, (tail, d), 0) == 0
    sels = []
    for g in range(n_grp):
        def blocks(scr, new):
            new_blk = jnp.where(first_row, jnp.broadcast_to(new[g:g + 1], (tail, d)), 0.0)
            return jnp.concatenate([scr[:, g * d:(g + 1) * d], scr[:, gw + g * d:gw + (g + 1) * d],
                                    new_blk], axis=0).astype(BF16)
        kc = blocks(kc_scr, kn_cmp)
        vc = blocks(vc_scr, vn_cmp)
        s = lax.dot_general(q_ref[g], kc, (((1,), (1,)), ((), ())), preferred_element_type=F32)
        s = s * scale - slope_ref[g] * distf
        s = jnp.where(valid, s, NEG_INF)
        m = jnp.max(s, axis=-1, keepdims=True)
        e = jnp.where(valid, jnp.exp(s - m), 0.0)
        p = e / jnp.maximum(jnp.sum(e, axis=-1, keepdims=True), TINY)
        o_ref[g] = jnp.dot(p.astype(BF16), vc, preferred_element_type=F32)
        score = jnp.sum(p, axis=0, keepdims=True)
        score = jnp.where(forced, NSA_FORCED, score)
        score = jnp.where(causal, score, -1.0)
        score = jnp.where(real, score, -2.0)
        sels.append(_rank_select(score, nid, n_sel, ids))
    sel = jnp.concatenate(sels + [jnp.zeros((8 - n_grp, nbp), F32)], axis=0)
    km = jnp.dot(sel.astype(BF16), e_ref[...], preferred_element_type=F32)
    for g in range(n_grp):
        mask_ref[g] = km[g:g + 1]


def nsa_cmp_sample(q, kpool, vpool, kc_new, vc_new, wk0, wv0, slopes, page_table, pos):
    b, n_grp, n_rep, d = q.shape
    n_pg = page_table.shape[1]
    nb = 2 * n_pg + 1
    nbp = _round_up(nb, LANE)
    total = n_pg * PAGE
    ids = [2 * s for s in range(n_pg)] + [2 * s + 1 for s in range(n_pg)] + [2 * n_pg]
    slot_of_block = {bid: s for s, bid in enumerate(ids)}
    key_slot = jnp.asarray([slot_of_block[kp // NSA_BLK] for kp in range(total)]
                           + [slot_of_block[2 * n_pg]] + [-1] * (LANE - 1), jnp.int32)
    expand = (jnp.arange(nbp, dtype=jnp.int32)[:, None] == key_slot[None, :]).astype(BF16)
    npool = kpool.shape[0]
    full = lambda shape: pl.BlockSpec(shape, lambda bb, pt: tuple(0 for _ in shape))
    return pl.pallas_call(
        functools.partial(_nsa_cmp_sample_kernel, n_grp=n_grp, n_rep=n_rep, n_pg=n_pg, nbp=nbp,
                          n_sel=min(NSA_TOPK, nb), pos=pos, scale=d ** -0.5, ids=ids),
        grid_spec=pltpu.PrefetchScalarGridSpec(
            num_scalar_prefetch=1, grid=(b,),
            in_specs=[pl.BlockSpec((None, n_grp, n_rep, d), lambda bb, pt: (bb, 0, 0, 0)),
                      full((npool, 2 * n_grp * d)), full((npool, 2 * n_grp * d)),
                      pl.BlockSpec((None, 8, d), lambda bb, pt: (bb, 0, 0)),
                      pl.BlockSpec((None, 8, d), lambda bb, pt: (bb, 0, 0)),
                      full((d, d)), full((d, d)), full((n_grp, n_rep, 1)), full((nbp, total + LANE))],
            out_specs=[pl.BlockSpec((None, n_grp, n_rep, d), lambda bb, pt: (bb, 0, 0, 0)),
                       pl.BlockSpec((None, n_grp, 1, total + LANE), lambda bb, pt: (bb, 0, 0, 0))],
            scratch_shapes=[pltpu.VMEM((n_pg, 2 * n_grp * d), F32), pltpu.VMEM((n_pg, 2 * n_grp * d), F32)]),
        out_shape=[jax.ShapeDtypeStruct((b, n_grp, n_rep, d), F32),
                   jax.ShapeDtypeStruct((b, n_grp, 1, total + LANE), F32)],
        compiler_params=_cparams(("arbitrary",), 56),
        name="nsa_cmp_sample",
    )(page_table, q, kpool, vpool, kc_new, vc_new, wk0, wv0, slopes.astype(F32).reshape(n_grp, n_rep, 1), expand)


def _pool_heads(cache, heads, width):
    n = cache.shape[0]
    return cache.reshape(n, PAGE, heads, width).transpose(0, 2, 1, 3).astype(BF16)


def diff_layer(xp, xs, hp, hs, bsz, tlen, past, cache_k, cache_v, page_table, w_in, lam_vecs, subln, w_o,
               layer_idx):
    lam_init = 0.8 - 0.6 * math.exp(-0.3 * layer_idx)
    slopes = alibi_slopes(A_HEADS, A_KV)
    n_rep = A_HEADS // A_KV
    a_q, a_kw = A_HEADS * 2 * A_HD, A_KV * 2 * A_HD
    w_in_b, w_o_b = w_in.astype(BF16), w_o.astype(BF16)
    diff = (lam_vecs, subln, lam_init)

    proj = matmul(hp, w_in_b).reshape(bsz, tlen, a_q + 2 * a_kw)
    q = proj[..., :a_q].reshape(bsz, tlen, A_KV, n_rep, 2, A_HD).transpose(0, 2, 4, 3, 1, 5).astype(BF16)
    kp = proj[..., a_q:a_q + a_kw].reshape(bsz, tlen, A_KV, 2, A_HD)
    vp = proj[..., a_q + a_kw:].reshape(bsz, tlen, A_KV, 2 * A_HD)
    o = flash_prompt(q, kp.transpose(0, 2, 3, 1, 4).astype(BF16), vp.transpose(0, 2, 1, 3).astype(BF16),
                     slopes, diff=diff, out_dtype=BF16)
    xp = matmul(o.reshape(bsz * tlen, a_q), w_o_b, res=xp)

    nb = hs.shape[0]
    proj_s = matmul(hs, w_in_b)
    qs = proj_s[:, :a_q].reshape(nb, A_KV, n_rep, 2, A_HD).transpose(0, 1, 3, 2, 4).astype(BF16)
    ks = proj_s[:, a_q:a_q + a_kw].reshape(nb, 1, A_KV, 2, A_HD)
    vs = proj_s[:, a_q + a_kw:].reshape(nb, 1, A_KV, 2 * A_HD)
    os_ = decode_attn(qs, _pool_heads(cache_k, A_KV * 2, A_HD), _pool_heads(cache_v, A_KV, 2 * A_HD),
                      ks.reshape(nb, A_KV * 2, 1, A_HD).astype(BF16), vs.reshape(nb, A_KV, 1, 2 * A_HD).astype(BF16),
                      slopes, pos=past, page_table=page_table, pages_per_step=4, diff=diff, out_dtype=BF16)
    xs = matmul(os_.reshape(nb, a_q), w_o_b, res=xs)
    return xp, xs, (kp, ks, vp, vs)


def _cmp_weight(w_cmp):
    blk, d, _ = w_cmp.shape
    eye = jnp.eye(B_KV, dtype=w_cmp.dtype)
    return jnp.einsum('jde,gh->jgdhe', w_cmp, eye).reshape(blk * B_KV * d, B_KV * d).astype(BF16)


def nsa_layer(xp, xs, hp, hs, bsz, tlen, past, cache_kc, cache_vc, cache_ks, cache_vs, buf_kw, buf_vw,
              page_table, w_in, b_gate, w_cmp_k, w_cmp_v, w_o):
    slopes = alibi_slopes(B_HEADS, B_KV)
    n_rep = B_HEADS // B_KV
    b_q, b_kw = B_HEADS * B_HD, B_KV * B_HD
    n_in = w_in.shape[1]
    n_pad = _round_up(n_in, 1024)
    gate_col = b_q + 6 * b_kw
    w_in_b = _pad_cols(w_in, n_pad).astype(BF16)
    w_o_b = w_o.astype(BF16)
    wck, wcv = _cmp_weight(w_cmp_k), _cmp_weight(w_cmp_v)

    proj = matmul(hp, w_in_b)
    proj3 = proj.reshape(bsz, tlen, n_pad)
    q = proj3[..., :b_q].reshape(bsz, tlen, B_KV, n_rep, B_HD).transpose(0, 2, 3, 1, 4).astype(BF16)
    kv = [proj3[..., b_q + i * b_kw:b_q + (i + 1) * b_kw].reshape(bsz, tlen, B_KV, B_HD) for i in range(6)]
    heads = lambda a: a.transpose(0, 2, 1, 3).astype(BF16)
    nblk = tlen // NSA_BLK
    cmp_in = lambda a: a.reshape(bsz * nblk, NSA_BLK * b_kw).astype(BF16)
    kc_cmp = matmul(cmp_in(kv[0]), wck).reshape(bsz, nblk, B_KV, B_HD).transpose(0, 2, 1, 3).astype(BF16)
    vc_cmp = matmul(cmp_in(kv[1]), wcv).reshape(bsz, nblk, B_KV, B_HD).transpose(0, 2, 1, 3).astype(BF16)
    o_cmp, sel = nsa_cmp_prompt(q, kc_cmp, vc_cmp, slopes)
    expand = (jnp.arange(_round_up(nblk, LANE), dtype=jnp.int32)[:, None]
              == (jnp.arange(tlen, dtype=jnp.int32) // NSA_BLK)[None, :]).astype(BF16)
    q6 = q[:, :, None]
    o_sel = flash_prompt(q6, heads(kv[2])[:, :, None], heads(kv[3]), slopes, sel=(sel, expand))
    o_win = flash_prompt(q6, heads(kv[4])[:, :, None], heads(kv[5]), slopes, window=NSA_WIN)
    m = bsz * tlen
    o = nsa_combine(o_cmp.reshape(m, b_q), o_sel.reshape(m, b_q), o_win.reshape(m, b_q), proj, b_gate, gate_col)
    xp = matmul(o, w_o_b, res=xp)

    nb = hs.shape[0]
    proj_s = matmul(hs, w_in_b)
    qs = proj_s[:, :b_q].reshape(nb, B_KV, n_rep, B_HD).astype(BF16)
    kvs = [proj_s[:, b_q + i * b_kw:b_q + (i + 1) * b_kw].reshape(nb, 1, B_KV, B_HD) for i in range(6)]
    npool = cache_kc.shape[0]
    pool_in = lambda c: c.reshape(npool * 2, NSA_BLK * b_kw).astype(BF16)
    kpool = matmul(pool_in(cache_kc), wck).reshape(npool, 2 * b_kw)
    vpool = matmul(pool_in(cache_vc), wcv).reshape(npool, 2 * b_kw)
    new8 = lambda a: jnp.pad(a.reshape(nb, B_KV, B_HD), ((0, 0), (0, 8 - B_KV), (0, 0)))
    o_cmp_s, keymask = nsa_cmp_sample(qs, kpool, vpool, new8(kvs[0]), new8(kvs[1]),
                                      w_cmp_k[0].astype(BF16), w_cmp_v[0].astype(BF16), slopes, page_table, past)
    qs5 = qs[:, :, None]
    new_k = lambda a: a.reshape(nb, B_KV, 1, B_HD).astype(BF16)
    o_sel_s = decode_attn(qs5, _pool_heads(cache_ks, B_KV, B_HD), _pool_heads(cache_vs, B_KV, B_HD),
                          new_k(kvs[2]), new_k(kvs[3]), slopes, pos=past, page_table=page_table,
                          pages_per_step=8, mask=keymask)
    o_win_s = decode_attn(qs5, heads(buf_kw), heads(buf_vw), new_k(kvs[4]), new_k(kvs[5]), slopes, pos=past,
                          window=NSA_WIN)
    os_ = nsa_combine(o_cmp_s.reshape(nb, b_q), o_sel_s.reshape(nb, b_q), o_win_s.reshape(nb, b_q), proj_s,
                      b_gate, gate_col)
    xs = matmul(os_, w_o_b, res=xs)

    wb = buf_kw.shape[1]
    wp = min(NSA_WIN, tlen)
    kw_all = jnp.concatenate([buf_kw, kvs[4]], axis=1)
    vw_all = jnp.concatenate([buf_vw, kvs[5]], axis=1)
    state = (kv[0], kvs[0], kv[1], kvs[1], kv[2], kvs[2], kv[3], kvs[3],
             kv[4][:, tlen - wp:], kw_all[:, -wb:], kv[5][:, tlen - wp:], vw_all[:, -wb:])
    return xp, xs, state


def swa_layer(xp, xs, hp, hs, bsz, tlen, past, buf_k, buf_v, w_in, b_in, sinks, w_o, b_o):
    slopes = alibi_slopes(C_HEADS, C_KV)
    sink = sinks.astype(F32).reshape(C_KV, C_HEADS // C_KV)
    n_rep = C_HEADS // C_KV
    c_q, c_kw = C_HEADS * C_HD, C_KV * C_HD
    w_in_b, w_o_b = w_in.astype(BF16), w_o.astype(BF16)
    heads = lambda a: a.transpose(0, 2, 1, 3).astype(BF16)

    proj = matmul(hp, w_in_b, bias=b_in).reshape(bsz, tlen, c_q + 2 * c_kw)
    q = proj[..., :c_q].reshape(bsz, tlen, C_KV, n_rep, C_HD).transpose(0, 2, 3, 1, 4).astype(BF16)
    kp = proj[..., c_q:c_q + c_kw].reshape(bsz, tlen, C_KV, C_HD)
    vp = proj[..., c_q + c_kw:].reshape(bsz, tlen, C_KV, C_HD)
    o = flash_prompt(q[:, :, None], heads(kp)[:, :, None], heads(vp), slopes, window=C_WIN, sink=sink,
                     out_dtype=BF16, head_major_out=True)
    o = o.transpose(0, 3, 1, 2, 4).reshape(bsz * tlen, c_q)
    xp = matmul(o, w_o_b, bias=b_o, res=xp)

    nb = hs.shape[0]
    proj_s = matmul(hs, w_in_b, bias=b_in)
    qs = proj_s[:, :c_q].reshape(nb, C_KV, 1, n_rep, C_HD).astype(BF16)
    ks = proj_s[:, c_q:c_q + c_kw].reshape(nb, 1, C_KV, C_HD)
    vs = proj_s[:, c_q + c_kw:].reshape(nb, 1, C_KV, C_HD)
    os_ = decode_attn(qs, heads(buf_k), heads(buf_v), ks.reshape(nb, C_KV, 1, C_HD).astype(BF16),
                      vs.reshape(nb, C_KV, 1, C_HD).astype(BF16), slopes, pos=past, window=C_WIN, sink=sink,
                      out_dtype=BF16)
    xs = matmul(os_.reshape(nb, c_q), w_o_b, bias=b_o, res=xs)

    wb = buf_k.shape[1]
    wp = min(C_WIN, tlen)
    k_all = jnp.concatenate([buf_k, ks], axis=1)
    v_all = jnp.concatenate([buf_v, vs], axis=1)
    return xp, xs, (kp[:, tlen - wp:], k_all[:, -wb:], vp[:, tlen - wp:], v_all[:, -wb:])


def kernel(x_prompt, x_sample, cache_l0_k, cache_l0_v, cache_l1_kc, cache_l1_vc, cache_l1_ks, cache_l1_vs, state_l1_kw, state_l1_vw, state_l2_k, state_l2_v, cache_l3_k, cache_l3_v, page_table, norm_ffa, norm_mix, norm_ffb, norm_final, ffa_w13, ffa_w2, ffb_w13, ffb_w2, l0_w_in, l0_lam, l0_subln, l0_w_o, l1_w_in, l1_b_gate, l1_w_cmp_k, l1_w_cmp_v, l1_w_o, l2_w_in, l2_b_in, l2_sinks, l2_w_o, l2_b_o, l3_w_in, l3_lam, l3_subln, l3_w_o):
    bsz, tlen, dm = x_prompt.shape
    nb, dec_seq, _ = x_sample.shape
    assert dec_seq == 1 and dm == D_MODEL
    past = page_table.shape[1] * PAGE
    xp = x_prompt.reshape(bsz * tlen, dm)
    xs = x_sample.reshape(nb, dm)
    a_params = {0: (cache_l0_k, cache_l0_v, l0_w_in, l0_lam, l0_subln, l0_w_o),
                3: (cache_l3_k, cache_l3_v, l3_w_in, l3_lam, l3_subln, l3_w_o)}
    state = {}
    for i in range(4):
        xp, xs = ffn_block(xp, xs, norm_ffa[i], ffa_w13[i], ffa_w2[i])
        hp = rmsnorm(xp, norm_mix[i], BF16)
        hs = rmsnorm(xs, norm_mix[i], BF16)
        kind = i % 3
        if kind == 0:
            ck, cv, w_in, lam, subln, w_o = a_params[i]
            xp, xs, st = diff_layer(xp, xs, hp, hs, bsz, tlen, past, ck, cv, page_table, w_in, lam, subln, w_o, i)
        elif kind == 1:
            xp, xs, st = nsa_layer(xp, xs, hp, hs, bsz, tlen, past, cache_l1_kc, cache_l1_vc, cache_l1_ks,
                                   cache_l1_vs, state_l1_kw, state_l1_vw, page_table, l1_w_in, l1_b_gate,
                                   l1_w_cmp_k, l1_w_cmp_v, l1_w_o)
        else:
            xp, xs, st = swa_layer(xp, xs, hp, hs, bsz, tlen, past, state_l2_k, state_l2_v, l2_w_in, l2_b_in,
                                   l2_sinks, l2_w_o, l2_b_o)
        state[i] = st
        xp, xs = ffn_block(xp, xs, norm_ffb[i], ffb_w13[i], ffb_w2[i])
    y_prompt = rmsnorm(xp, norm_final, F32).reshape(bsz, tlen, dm)
    y_sample = rmsnorm(xs, norm_final, F32).reshape(nb, 1, dm)
    return (y_prompt, y_sample) + tuple(state[0]) + tuple(state[1]) + tuple(state[2]) + tuple(state[3])
```

```python
import functools
import math

import jax
import jax.numpy as jnp
from jax import lax
from jax.experimental import pallas as pl
from jax.experimental.pallas import tpu as pltpu

NORM_EPS = 1e-6
NEG_INF = -1e30
TINY = 1e-30

D_MODEL = 4096
PAGE = 128
A_HD, A_HEADS, A_KV = 128, 16, 4
B_HD, B_HEADS, B_KV = 128, 32, 2
NSA_BLK, NSA_TOPK, NSA_WIN, NSA_FORCED = 64, 16, 512, 1e4
C_HD, C_HEADS, C_KV, C_WIN = 64, 64, 8, 128

LANE = 128
QT = 128
KT = 512
FFN_TN = 256
MIB = 1024 * 1024
BF16 = jnp.bfloat16
F32 = jnp.float32


def _cparams(sem, vmem_mib):
    return pltpu.CompilerParams(dimension_semantics=sem, vmem_limit_bytes=vmem_mib * MIB)


def _round_up(x, m):
    return -(-x // m) * m


def _pick(n, prefs):
    for p in prefs:
        if n % p == 0:
            return p
    return n


def _rep(x, width):
    return x if width == LANE else jnp.concatenate([x] * (width // LANE), axis=1)


def _rms_kernel(x_ref, g_ref, o_ref):
    x = x_ref[...]
    ms = jnp.mean(x * x, axis=-1, keepdims=True)
    o_ref[...] = (x * lax.rsqrt(ms + NORM_EPS) * g_ref[...]).astype(o_ref.dtype)


def rmsnorm(x, g, out_dtype):
    m, d = x.shape
    tm = _pick(m, (256, 128, 64, 32, 16, 8))
    return pl.pallas_call(
        _rms_kernel,
        grid=(m // tm,),
        in_specs=[pl.BlockSpec((tm, d), lambda i: (i, 0)), pl.BlockSpec((1, d), lambda i: (0, 0))],
        out_specs=pl.BlockSpec((tm, d), lambda i: (i, 0)),
        out_shape=jax.ShapeDtypeStruct((m, d), out_dtype),
        compiler_params=_cparams(("parallel",), 40),
        name="rmsnorm",
    )(x, g.reshape(1, d).astype(F32))


def _mm_kernel(*refs, nk, has_bias, has_res, res_scale):
    it = iter(refs)
    a_ref, w_ref = next(it), next(it)
    b_ref = next(it) if has_bias else None
    r_ref = next(it) if has_res else None
    o_ref = next(it)
    acc_ref = next(it) if nk > 1 else None

    def epilogue(acc):
        if has_bias:
            acc = acc + b_ref[...]
        if has_res:
            acc = r_ref[...] + res_scale * acc
        o_ref[...] = acc.astype(o_ref.dtype)

    d = jnp.dot(a_ref[...], w_ref[...], preferred_element_type=F32)
    if nk == 1:
        epilogue(d)
    else:
        k = pl.program_id(2)

        @pl.when(k == 0)
        def _():
            acc_ref[...] = d

        @pl.when(k > 0)
        def _():
            acc_ref[...] += d

        @pl.when(k == nk - 1)
        def _():
            epilogue(acc_ref[...])


def matmul(a, w, *, layer=None, bias=None, res=None, res_scale=1.0, out_dtype=F32):
    m, k = a.shape
    n = w.shape[-1]
    tm = _pick(m, (1024, 512, 256, 128, 64, 32, 16, 8))
    tk = k if k <= 4096 else _pick(k, (2816, 2048, 1408, 1024, 512, 256, 128))
    nk = k // tk
    tn = _pick(n, (1024, 512, 256, 128)) if nk > 1 else _pick(n, (512, 256, 128))
    if w.ndim == 3:
        w_spec = pl.BlockSpec((None, tk, tn), lambda i, j, kk: (layer, kk, j))
    else:
        w_spec = pl.BlockSpec((tk, tn), lambda i, j, kk: (kk, j))
    in_specs = [pl.BlockSpec((tm, tk), lambda i, j, kk: (i, kk)), w_spec]
    args = [a, w]
    if bias is not None:
        in_specs.append(pl.BlockSpec((1, tn), lambda i, j, kk: (0, j)))
        args.append(bias.reshape(1, n).astype(F32))
    if res is not None:
        in_specs.append(pl.BlockSpec((tm, tn), lambda i, j, kk: (i, j)))
        args.append(res)
    return pl.pallas_call(
        functools.partial(_mm_kernel, nk=nk, has_bias=bias is not None, has_res=res is not None,
                          res_scale=res_scale),
        grid=(m // tm, n // tn, nk),
        in_specs=in_specs,
        out_specs=pl.BlockSpec((tm, tn), lambda i, j, kk: (i, j)),
        out_shape=jax.ShapeDtypeStruct((m, n), out_dtype),
        scratch_shapes=[pltpu.VMEM((tm, tn), F32)] if nk > 1 else [],
        compiler_params=_cparams(("parallel", "parallel", "arbitrary"), 56),
        name="matmul",
    )(*args)


def _ffn1_kernel(a_ref, wg_ref, wu_ref, o_ref, *, n_real):
    j = pl.program_id(1)

    @pl.when(j < n_real)
    def _():
        a = a_ref[...]
        g = jnp.dot(a, wg_ref[...], preferred_element_type=F32)
        u = jnp.dot(a, wu_ref[...], preferred_element_type=F32)
        o_ref[...] = (g / (1.0 + jnp.exp(-g)) * u).astype(o_ref.dtype)

    @pl.when(j >= n_real)
    def _():
        o_ref[...] = jnp.zeros(o_ref.shape, o_ref.dtype)


def ffn1(a, w13, layer, f_pad):
    m, k = a.shape
    f = w13.shape[-1] // 2
    tn = _pick(f, (FFN_TN, LANE))
    n_real = f // tn
    tm = _pick(m, (2048, 1024, 512, 256, 128, 64, 32, 16, 8))
    clamp = lambda j: jnp.minimum(j, n_real - 1)
    return pl.pallas_call(
        functools.partial(_ffn1_kernel, n_real=n_real),
        grid=(m // tm, f_pad // tn),
        in_specs=[pl.BlockSpec((tm, k), lambda i, j: (i, 0)),
                  pl.BlockSpec((None, k, tn), lambda i, j: (layer, 0, clamp(j))),
                  pl.BlockSpec((None, k, tn), lambda i, j: (layer, 0, clamp(j) + n_real))],
        out_specs=pl.BlockSpec((tm, tn), lambda i, j: (i, j)),
        out_shape=jax.ShapeDtypeStruct((m, f_pad), BF16),
        compiler_params=_cparams(("parallel", "parallel"), 56),
        name="ffn1",
    )(a, w13, w13)


def ffn_block(xp, xs, g, w13, w2p, layer):
    outs = []
    for x in (xp, xs):
        h = rmsnorm(x, g, BF16)
        mid = ffn1(h, w13, layer, w2p.shape[1])
        outs.append(matmul(mid, w2p, layer=layer, res=x, res_scale=0.5))
    return outs


def alibi_slopes(n_heads, n_groups):
    i = jnp.arange(1, n_heads + 1, dtype=F32)
    return jnp.exp2(-8.0 * i / n_heads).reshape(n_groups, n_heads // n_groups)


def _diff_finish(o0, o1, lam_ref, subln_ref, lam_init):
    lv = lam_ref[...]
    lam = (jnp.exp(jnp.sum(lv[0:1] * lv[1:2], axis=-1, keepdims=True))
           - jnp.exp(jnp.sum(lv[2:3] * lv[3:4], axis=-1, keepdims=True)) + lam_init)
    o = o0 - lam * o1
    ms = jnp.mean(o * o, axis=-1, keepdims=True)
    return o * lax.rsqrt(ms + NORM_EPS) * subln_ref[...] * (1.0 - lam_init)


def _causal_kernel(*refs, n_maps, n_rep, d, dv, nkt, scale, has_sel, diff, lam_init):
    it = iter(refs)
    slope_ref = next(it)
    q_ref, k_ref, v_ref = next(it), next(it), next(it)
    sel_ref, e_ref = (next(it), next(it)) if has_sel else (None, None)
    lam_ref, subln_ref = (next(it), next(it)) if diff else (None, None)
    o_ref = next(it)
    q_scr, m_ref, l_ref, acc_ref, e_scr, a_scr = (next(it) for _ in range(6))

    tq, tk = QT, KT
    ratio = tk // tq
    g = pl.program_id(1)
    i = pl.program_id(2)
    kt = pl.program_id(3)
    diag = i // ratio

    @pl.when(kt == 0)
    def _():
        m_ref[...] = jnp.full(m_ref.shape, NEG_INF, F32)
        l_ref[...] = jnp.zeros(l_ref.shape, F32)
        acc_ref[...] = jnp.zeros(acc_ref.shape, F32)
        for c in range(n_maps):
            for r in range(n_rep):
                col = (r * n_maps + c) * d
                q_scr[c, r * tq:(r + 1) * tq, :] = q_ref[:, col:col + d].astype(BF16)

    def step(causal):
        kposf = (kt * tk - i * tq + lax.broadcasted_iota(jnp.int32, (1, tk), 1)).astype(F32)
        mask = None
        if causal:
            mask = (kt * tk + lax.broadcasted_iota(jnp.int32, (tq, tk), 1)
                    <= i * tq + lax.broadcasted_iota(jnp.int32, (tq, tk), 0))
        if has_sel:
            picked = jnp.dot(sel_ref[...].astype(BF16), e_ref[...], preferred_element_type=F32) > 0.5
            mask = picked if mask is None else mask & picked
        kb = k_ref[...].astype(BF16)
        vb = v_ref[...].astype(BF16)
        for c in range(n_maps):
            s_all = lax.dot_general(q_scr[c], kb[:, c * d:(c + 1) * d], (((1,), (1,)), ((), ())),
                                    preferred_element_type=F32)
            for r in range(n_rep):
                sl = slice(r * tq, (r + 1) * tq)
                s = s_all[sl] * scale + slope_ref[g * n_rep + r] * kposf
                if mask is not None:
                    s = jnp.where(mask, s, NEG_INF)
                m_old = m_ref[c, sl]
                m_new = jnp.maximum(m_old, jnp.max(s, axis=-1, keepdims=True))
                e = jnp.exp(s - _rep(m_new, tk))
                if mask is not None:
                    e = jnp.where(mask, e, 0.0)
                alpha = jnp.exp(m_old - m_new)
                l_ref[c, sl] = alpha * l_ref[c, sl] + jnp.sum(e, axis=-1, keepdims=True)
                m_ref[c, sl] = m_new
                a_scr[sl] = alpha
                e_scr[sl] = e.astype(BF16)
            acc_ref[c] = _rep(a_scr[...], dv) * acc_ref[c] + jnp.dot(e_scr[...], vb, preferred_element_type=F32)

    @pl.when(kt < diag)
    def _():
        step(False)

    @pl.when(kt == diag)
    def _():
        step(True)

    @pl.when(kt == nkt - 1)
    def _():
        o = acc_ref[0] / jnp.maximum(_rep(l_ref[0], dv), TINY)
        if diff:
            o = _diff_finish(o, acc_ref[1] / jnp.maximum(_rep(l_ref[1], dv), TINY), lam_ref, subln_ref, lam_init)
        for r in range(n_rep):
            o_ref[:, r * dv:(r + 1) * dv] = o[r * tq:(r + 1) * tq].astype(o_ref.dtype)


def causal_prompt(proj, slopes, *, n_grp, n_rep, n_maps, d, dv, q_col, k_col, v_col, sel=None, diff=None,
                  out_dtype=F32):
    b, tlen, _ = proj.shape
    tq, tk = QT, KT
    ratio = tk // tq
    nq, nkt = tlen // tq, tlen // tk
    rows = n_rep * tq
    qw, kw = n_rep * n_maps * d, n_maps * d
    kidx = lambda i, kt: jnp.minimum(kt, i // ratio)
    in_specs = [
        pl.BlockSpec(memory_space=pltpu.SMEM),
        pl.BlockSpec((None, tq, qw), lambda bb, gg, i, kt: (bb, i, q_col // qw + gg)),
        pl.BlockSpec((None, tk, kw), lambda bb, gg, i, kt: (bb, kidx(i, kt), k_col // kw + gg)),
        pl.BlockSpec((None, tk, dv), lambda bb, gg, i, kt: (bb, kidx(i, kt), v_col // dv + gg)),
    ]
    args = [slopes.reshape(-1).astype(F32), proj, proj, proj]
    if sel is not None:
        sel_mask, expand = sel
        nbp = sel_mask.shape[-1]
        in_specs.append(pl.BlockSpec((None, None, tq, nbp), lambda bb, gg, i, kt: (bb, gg, i, 0)))
        in_specs.append(pl.BlockSpec((nbp, tk), lambda bb, gg, i, kt: (0, kidx(i, kt))))
        args += [sel_mask, expand]
    lam_init = 0.0
    if diff is not None:
        lam_vecs, subln, lam_init = diff
        in_specs.append(pl.BlockSpec((4, A_HD), lambda bb, gg, i, kt: (0, 0)))
        in_specs.append(pl.BlockSpec((1, dv), lambda bb, gg, i, kt: (0, 0)))
        args += [lam_vecs.astype(F32), subln.reshape(1, dv).astype(F32)]
    return pl.pallas_call(
        functools.partial(_causal_kernel, n_maps=n_maps, n_rep=n_rep, d=d, dv=dv, nkt=nkt, scale=d ** -0.5,
                          has_sel=sel is not None, diff=diff is not None, lam_init=lam_init),
        grid=(b, n_grp, nq, nkt),
        in_specs=in_specs,
        out_specs=pl.BlockSpec((None, tq, n_rep * dv), lambda bb, gg, i, kt: (bb, i, gg)),
        out_shape=jax.ShapeDtypeStruct((b, tlen, n_grp * n_rep * dv), out_dtype),
        scratch_shapes=[pltpu.VMEM((n_maps, rows, d), BF16), pltpu.VMEM((n_maps, rows, LANE), F32),
                        pltpu.VMEM((n_maps, rows, LANE), F32), pltpu.VMEM((n_maps, rows, dv), F32),
                        pltpu.VMEM((rows, tk), BF16), pltpu.VMEM((rows, LANE), F32)],
        compiler_params=_cparams(("parallel", "parallel", "parallel", "arbitrary"), 48),
        name="causal_prompt",
    )(*args)


def _banded_kernel(*refs, n_rep, n_chunks, tq, d, dv, window, scale, has_sink, head_major):
    it = iter(refs)
    slope_ref = next(it)
    sink_ref = next(it) if has_sink else None
    q_ref, kp_ref, kc_ref, vp_ref, vc_ref, o_ref = (next(it) for _ in range(6))
    q_scr, e_scr = next(it), next(it)

    tk = 2 * tq
    if head_major:
        g, i = pl.program_id(1), pl.program_id(2)
        head0 = g * n_rep
    else:
        i = pl.program_id(2)
        head0 = pl.program_id(1) * n_rep
    for r in range(n_rep):
        if head_major:
            q_scr[r * tq:(r + 1) * tq, :] = q_ref[r]
        else:
            q_scr[r * tq:(r + 1) * tq, :] = q_ref[:, r * d:(r + 1) * d].astype(BF16)
    kb = jnp.concatenate([kp_ref[...], kc_ref[...]], axis=0).astype(BF16)
    vb = jnp.concatenate([vp_ref[...], vc_ref[...]], axis=0).astype(BF16)
    s_all = lax.dot_general(q_scr[...], kb, (((1,), (1,)), ((), ())), preferred_element_type=F32)
    row = lax.broadcasted_iota(jnp.int32, (tq, tk), 0)
    col = lax.broadcasted_iota(jnp.int32, (tq, tk), 1)
    dist = row + tq - col
    mask = (dist >= 0) & (dist <= window) & ((i - 1) * tq + col >= 0)
    kposf = (lax.broadcasted_iota(jnp.int32, (1, tk), 1) - tq).astype(F32)
    rowf = lax.broadcasted_iota(jnp.int32, (tq, 1), 0).astype(F32)
    for r in range(n_rep):
        sl = slice(r * tq, (r + 1) * tq)
        slope = slope_ref[head0 + r]
        s = s_all[sl] * scale + slope * kposf
        s = jnp.where(mask, s, NEG_INF)
        m = jnp.max(s, axis=-1, keepdims=True)
        if has_sink:
            sink = sink_ref[head0 + r] + slope * rowf
            m = jnp.maximum(m, sink)
        e = jnp.where(mask, jnp.exp(s - m), 0.0)
        den = jnp.sum(e, axis=-1, keepdims=True)
        if has_sink:
            den = den + jnp.exp(sink - m)
        e_scr[sl] = (e / jnp.maximum(den, TINY)).astype(BF16)
    o = jnp.dot(e_scr[...], vb, preferred_element_type=F32)
    for r in range(n_rep):
        if head_major:
            o_ref[r] = o[r * tq:(r + 1) * tq].astype(o_ref.dtype)
        else:
            o_ref[:, r * dv:(r + 1) * dv] = o[r * tq:(r + 1) * tq].astype(o_ref.dtype)


def banded_prompt(q, k, v, slopes, *, window, tq, n_grp, n_rep, d, dv, q_col=0, k_col=0, v_col=0, rep_chunk=None,
                  sink=None, out_dtype=F32):
    head_major = rep_chunk is None
    assert window <= tq
    tlen = q.shape[3] if head_major else q.shape[1]
    b = q.shape[0]
    nq = tlen // tq
    prev = lambda i: jnp.maximum(i - 1, 0)
    smem = pl.BlockSpec(memory_space=pltpu.SMEM)
    in_specs, args = [smem], [slopes.reshape(-1).astype(F32)]
    if sink is not None:
        in_specs.append(smem)
        args.append(sink.reshape(-1).astype(F32))
    if head_major:
        rc, n_chunks = n_rep, 1
        grid = (b, n_grp, nq)
        in_specs += [pl.BlockSpec((None, None, rc, tq, d), lambda bb, gg, i: (bb, gg, 0, i, 0)),
                     pl.BlockSpec((None, None, tq, d), lambda bb, gg, i: (bb, gg, prev(i), 0)),
                     pl.BlockSpec((None, None, tq, d), lambda bb, gg, i: (bb, gg, i, 0)),
                     pl.BlockSpec((None, None, tq, dv), lambda bb, gg, i: (bb, gg, prev(i), 0)),
                     pl.BlockSpec((None, None, tq, dv), lambda bb, gg, i: (bb, gg, i, 0))]
        out_spec = pl.BlockSpec((None, None, rc, tq, dv), lambda bb, gg, i: (bb, gg, 0, i, 0))
        out_shape = jax.ShapeDtypeStruct((b, n_grp, n_rep, tlen, dv), out_dtype)
    else:
        rc = rep_chunk
        n_chunks = n_rep // rc
        grid = (b, n_grp * n_chunks, nq)
        kcol = lambda gc: k_col // d + gc // n_chunks
        vcol = lambda gc: v_col // dv + gc // n_chunks
        in_specs += [pl.BlockSpec((None, tq, rc * d), lambda bb, gc, i: (bb, i, q_col // (rc * d) + gc)),
                     pl.BlockSpec((None, tq, d), lambda bb, gc, i: (bb, prev(i), kcol(gc))),
                     pl.BlockSpec((None, tq, d), lambda bb, gc, i: (bb, i, kcol(gc))),
                     pl.BlockSpec((None, tq, dv), lambda bb, gc, i: (bb, prev(i), vcol(gc))),
                     pl.BlockSpec((None, tq, dv), lambda bb, gc, i: (bb, i, vcol(gc)))]
        out_spec = pl.BlockSpec((None, tq, rc * dv), lambda bb, gc, i: (bb, i, gc))
        out_shape = jax.ShapeDtypeStruct((b, tlen, n_grp * n_rep * dv), out_dtype)
    args += [q, k, k, v, v]
    return pl.pallas_call(
        functools.partial(_banded_kernel, n_rep=rc, n_chunks=n_chunks, tq=tq, d=d, dv=dv, window=window,
                          scale=d ** -0.5, has_sink=sink is not None, head_major=head_major),
        grid=grid,
        in_specs=in_specs,
        out_specs=out_spec,
        out_shape=out_shape,
        scratch_shapes=[pltpu.VMEM((rc * tq, d), BF16), pltpu.VMEM((rc * tq, 2 * tq), BF16)],
        compiler_params=_cparams(("parallel", "parallel", "parallel"), 56),
        name="banded_prompt",
    )(*args)


def _rank_select(score, block_id, n_sel, ids):
    rank = jnp.zeros(score.shape, F32)
    for m, bid in enumerate(ids):
        cm = score[:, m:m + 1]
        beats = (cm > score) | ((cm == score) & (bid < block_id))
        rank = rank + jnp.where(beats, 1.0, 0.0)
    return jnp.where(rank < n_sel, 1.0, 0.0)


def _nsa_cmp_prompt_kernel(slope_ref, q_ref, kc_ref, vc_ref, o_ref, sel_ref, q_scr, *, n_rep, nb, n_sel, scale):
    t = QT
    d = kc_ref.shape[-1]
    nbp = kc_ref.shape[0]
    g = pl.program_id(1)
    i = pl.program_id(2)
    for r in range(n_rep):
        q_scr[r * t:(r + 1) * t, :] = q_ref[:, r * d:(r + 1) * d].astype(BF16)
    s_all = lax.dot_general(q_scr[...], kc_ref[...], (((1,), (1,)), ((), ())), preferred_element_type=F32)
    vc = vc_ref[...]
    pos = i * t + lax.broadcasted_iota(jnp.int32, (t, nbp), 0)
    blk = lax.broadcasted_iota(jnp.int32, (t, nbp), 1)
    real = blk < nb
    dist = pos - (blk * NSA_BLK + NSA_BLK - 1)
    distf = jnp.where(real, dist, 0).astype(F32)
    valid = (dist >= 0) & real
    score = jnp.zeros((t, nbp), F32)
    for r in range(n_rep):
        s = s_all[r * t:(r + 1) * t] * scale - slope_ref[g * n_rep + r] * distf
        s = jnp.where(valid, s, NEG_INF)
        m = jnp.max(s, axis=-1, keepdims=True)
        e = jnp.where(valid, jnp.exp(s - m), 0.0)
        p = e / jnp.maximum(jnp.sum(e, axis=-1, keepdims=True), TINY)
        score = score + p
        o_ref[:, r * d:(r + 1) * d] = jnp.dot(p.astype(BF16), vc, preferred_element_type=F32)
    cur = pos // NSA_BLK
    forced = (blk == 0) | (blk == cur) | (blk == cur - 1)
    causal = blk * NSA_BLK <= pos
    score = jnp.where(forced, NSA_FORCED, score)
    score = jnp.where(causal, score, -1.0)
    score = jnp.where(real, score, -2.0)
    sel_ref[...] = _rank_select(score, blk, n_sel, range(nb))


def nsa_cmp_prompt(proj, kc_cmp, vc_cmp, slopes, *, n_grp, n_rep, d):
    b, tlen, _ = proj.shape
    nb = kc_cmp.shape[2]
    nbp = _round_up(nb, LANE)
    kc_cmp = jnp.pad(kc_cmp, ((0, 0), (0, 0), (0, nbp - nb), (0, 0)))
    vc_cmp = jnp.pad(vc_cmp, ((0, 0), (0, 0), (0, nbp - nb), (0, 0)))
    t = QT
    return pl.pallas_call(
        functools.partial(_nsa_cmp_prompt_kernel, n_rep=n_rep, nb=nb, n_sel=min(NSA_TOPK, nb), scale=d ** -0.5),
        grid=(b, n_grp, tlen // t),
        in_specs=[pl.BlockSpec(memory_space=pltpu.SMEM),
                  pl.BlockSpec((None, t, n_rep * d), lambda bb, gg, i: (bb, i, gg)),
                  pl.BlockSpec((None, None, nbp, d), lambda bb, gg, i: (bb, gg, 0, 0)),
                  pl.BlockSpec((None, None, nbp, d), lambda bb, gg, i: (bb, gg, 0, 0))],
        out_specs=[pl.BlockSpec((None, t, n_rep * d), lambda bb, gg, i: (bb, i, gg)),
                   pl.BlockSpec((None, None, t, nbp), lambda bb, gg, i: (bb, gg, i, 0))],
        out_shape=[jax.ShapeDtypeStruct((b, tlen, n_grp * n_rep * d), F32),
                   jax.ShapeDtypeStruct((b, n_grp, tlen, nbp), F32)],
        scratch_shapes=[pltpu.VMEM((n_rep * t, d), BF16)],
        compiler_params=_cparams(("parallel", "parallel", "parallel"), 48),
        name="nsa_cmp_prompt",
    )(slopes.reshape(-1).astype(F32), proj, kc_cmp, vc_cmp)


def _nsa_combine_kernel(oc_ref, os_ref, ow_ref, gl_ref, bg_ref, o_ref, *, n_heads, d):
    gates = 1.0 / (1.0 + jnp.exp(-(gl_ref[...] + bg_ref[...])))
    for h in range(n_heads):
        sl = slice(h * d, (h + 1) * d)
        o = (oc_ref[:, sl] * gates[:, 3 * h:3 * h + 1] + os_ref[:, sl] * gates[:, 3 * h + 1:3 * h + 2]
             + ow_ref[:, sl] * gates[:, 3 * h + 2:3 * h + 3])
        o_ref[:, sl] = o.astype(o_ref.dtype)


def nsa_combine(o_cmp, o_sel, o_win, proj, b_gate, gate_col):
    m, n = o_cmp.shape
    tm = _pick(m, (256, 128, 64, 32, 16, 8))
    blk = lambda: pl.BlockSpec((tm, n), lambda i: (i, 0))
    bg = jnp.pad(b_gate.astype(F32), (0, LANE - b_gate.shape[0])).reshape(1, LANE)
    return pl.pallas_call(
        functools.partial(_nsa_combine_kernel, n_heads=B_HEADS, d=B_HD),
        grid=(m // tm,),
        in_specs=[blk(), blk(), blk(),
                  pl.BlockSpec((tm, LANE), lambda i: (i, gate_col // LANE)),
                  pl.BlockSpec((1, LANE), lambda i: (0, 0))],
        out_specs=blk(),
        out_shape=jax.ShapeDtypeStruct((m, n), BF16),
        compiler_params=_cparams(("parallel",), 48),
        name="nsa_combine",
    )(o_cmp, o_sel, o_win, proj, bg)


def _decode_kernel(*refs, n_grp, n_maps, n_rep, dv, n_pages, n_steps, paged, chunk, kpos0, pos, window,
                   scale, has_mask, has_sink, diff, lam_init):
    it = iter(refs)
    if paged:
        next(it)
    q_ref = next(it)
    k_refs = [next(it) for _ in range(n_pages)]
    v_refs = [next(it) for _ in range(n_pages)]
    kn_ref, vn_ref, slope_ref = next(it), next(it), next(it)
    sink_ref = next(it) if has_sink else None
    mask_ref, maskn_ref = (next(it), next(it)) if has_mask else (None, None)
    lam_ref, subln_ref = (next(it), next(it)) if diff else (None, None)
    o_ref = next(it)
    m_ref, l_ref, acc_ref = next(it), next(it), next(it)

    rows = n_maps * n_rep
    hk = n_grp * n_maps
    vh = max(dv // LANE, 1)
    hv = vh * n_grp
    npos = chunk // n_pages
    step = pl.program_id(1)

    @pl.when(step == 0)
    def _():
        if has_sink:
            m_ref[...] = sink_ref[...]
            l_ref[...] = jnp.ones(l_ref.shape, F32)
        else:
            m_ref[...] = jnp.full(m_ref.shape, NEG_INF, F32)
            l_ref[...] = jnp.zeros(l_ref.shape, F32)
        acc_ref[...] = jnp.zeros(acc_ref.shape, F32)

    kpos = kpos0 + step * chunk + lax.broadcasted_iota(jnp.int32, (1, chunk), 1)
    dist = pos - kpos
    ok = (dist >= 0) & (kpos >= 0)
    if window is not None:
        ok = ok & (dist <= window)
    distf = dist.astype(F32)
    for g in range(n_grp):
        okf = jnp.where(ok, 1.0, 0.0)
        if has_mask:
            okf = okf * mask_ref[g]
        okg = jnp.broadcast_to(okf, (rows, chunk)) > 0.5
        kcat = jnp.concatenate(
            [jnp.concatenate([r[pl.ds(g * n_maps + c, npos, stride=hk), :] for c in range(n_maps)], axis=1)
             for r in k_refs], axis=0).astype(BF16)
        vcat = jnp.concatenate(
            [jnp.concatenate([r[pl.ds(h * n_grp + g, npos, stride=hv), :] for h in range(vh)], axis=1)
             for r in v_refs], axis=0).astype(BF16)
        s = lax.dot_general(q_ref[g], kcat, (((1,), (1,)), ((), ())), preferred_element_type=F32)
        s = s * scale - slope_ref[g] * distf
        s = jnp.where(okg, s, NEG_INF)
        m_old = m_ref[g]
        m_new = jnp.maximum(m_old, jnp.max(s, axis=-1, keepdims=True))
        alpha = jnp.exp(m_old - m_new)
        e = jnp.where(okg, jnp.exp(s - m_new), 0.0)
        l_ref[g] = alpha * l_ref[g] + jnp.sum(e, axis=-1, keepdims=True)
        acc_ref[g] = alpha * acc_ref[g] + jnp.dot(e.astype(BF16), vcat, preferred_element_type=F32)
        m_ref[g] = m_new

    @pl.when(step == n_steps - 1)
    def _():
        for g in range(n_grp):
            kn = kn_ref[g].astype(BF16).astype(F32)
            vn = vn_ref[g].astype(BF16).astype(F32)
            s = jnp.sum(q_ref[g].astype(F32) * kn, axis=-1, keepdims=True) * scale
            if has_mask:
                okn = maskn_ref[g][:, 0:1] > 0.5
                s = jnp.where(okn, s, NEG_INF)
            m_old = m_ref[g]
            m_new = jnp.maximum(m_old, s)
            alpha = jnp.exp(m_old - m_new)
            e = jnp.exp(s - m_new)
            if has_mask:
                e = jnp.where(okn, e, 0.0)
            den = alpha * l_ref[g] + e
            o = (alpha * acc_ref[g] + e * vn) / jnp.maximum(den, TINY)
            if diff:
                o = _diff_finish(o[0:n_rep], o[n_rep:rows], lam_ref, subln_ref, lam_init)
            o_ref[g] = o.astype(o_ref.dtype)


def _block_diag_q(q):
    b, g, c, r, d = q.shape
    eye = jnp.eye(c, dtype=q.dtype)
    return jnp.einsum('bgcrd,ce->bgcred', q, eye).reshape(b, g, c * r, c * d)


def decode_attn(q, k_main, v_main, k_new, v_new, slopes, *, pos, page_table=None, pages_per_step=1,
                window=None, sink=None, mask=None, diff=None, out_dtype=F32):
    b, n_grp, n_maps, n_rep, d = q.shape
    dv = v_new.shape[-1]
    rows = n_maps * n_rep
    hk = n_grp * n_maps
    paged = page_table is not None
    if paged:
        n_pages = pages_per_step
        n_steps = page_table.shape[1] // n_pages
        chunk = n_pages * PAGE
        kpos0 = 0
        total = page_table.shape[1] * PAGE
    else:
        n_pages, n_steps = 1, 1
        chunk = k_main.shape[1] // hk
        kpos0 = pos - chunk
        total = chunk

    def idx(f):
        if paged:
            return lambda bb, st, pt: f(bb, st, pt)
        return lambda bb, st: f(bb, st, None)

    in_specs = [pl.BlockSpec((None, n_grp, rows, n_maps * d), idx(lambda bb, st, pt: (bb, 0, 0, 0)))]
    args = [_block_diag_q(q)]
    for main in (k_main, v_main):
        blk = (None,) + main.shape[1:]
        for p in range(n_pages):
            if paged:
                in_specs.append(pl.BlockSpec(blk, lambda bb, st, pt, p=p: (pt[bb, st * n_pages + p], 0, 0)))
            else:
                in_specs.append(pl.BlockSpec(blk, lambda bb, st: (bb, 0, 0)))
            args.append(main)
    col = lambda x: jnp.tile(x.astype(F32).reshape(n_grp, 1, n_rep), (1, n_maps, 1)).reshape(n_grp, rows, 1)
    in_specs += [pl.BlockSpec((None, n_grp, 1, n_maps * d), idx(lambda bb, st, pt: (bb, 0, 0, 0))),
                 pl.BlockSpec((None, n_grp, 1, dv), idx(lambda bb, st, pt: (bb, 0, 0, 0))),
                 pl.BlockSpec((n_grp, rows, 1), idx(lambda bb, st, pt: (0, 0, 0)))]
    args += [k_new, v_new, col(slopes)]
    if sink is not None:
        in_specs.append(pl.BlockSpec((n_grp, rows, 1), idx(lambda bb, st, pt: (0, 0, 0))))
        args.append(col(sink))
    if mask is not None:
        in_specs.append(pl.BlockSpec((None, n_grp, 1, chunk), idx(lambda bb, st, pt: (bb, 0, 0, st))))
        in_specs.append(pl.BlockSpec((None, n_grp, 1, LANE), idx(lambda bb, st, pt: (bb, 0, 0, total // LANE))))
        args += [mask, mask]
    lam_init = 0.0
    if diff is not None:
        lam_vecs, subln, lam_init = diff
        in_specs.append(pl.BlockSpec((4, A_HD), idx(lambda bb, st, pt: (0, 0))))
        in_specs.append(pl.BlockSpec((1, dv), idx(lambda bb, st, pt: (0, 0))))
        args += [lam_vecs.astype(F32), subln.reshape(1, dv).astype(F32)]
    out_spec = pl.BlockSpec((None, n_grp, n_rep, dv), idx(lambda bb, st, pt: (bb, 0, 0, 0)))
    scratch = [pltpu.VMEM((n_grp, rows, 1), F32), pltpu.VMEM((n_grp, rows, 1), F32),
               pltpu.VMEM((n_grp, rows, dv), F32)]
    body = functools.partial(
        _decode_kernel, n_grp=n_grp, n_maps=n_maps, n_rep=n_rep, dv=dv, n_pages=n_pages, n_steps=n_steps,
        paged=paged, chunk=chunk, kpos0=kpos0, pos=pos, window=window, scale=d ** -0.5,
        has_mask=mask is not None, has_sink=sink is not None, diff=diff is not None, lam_init=lam_init)
    out_shape = jax.ShapeDtypeStruct((b, n_grp, n_rep, dv), out_dtype)
    cp = _cparams(("parallel", "arbitrary"), 48)
    if paged:
        return pl.pallas_call(
            body,
            grid_spec=pltpu.PrefetchScalarGridSpec(num_scalar_prefetch=1, grid=(b, n_steps), in_specs=in_specs,
                                                   out_specs=out_spec, scratch_shapes=scratch),
            out_shape=out_shape, compiler_params=cp, name="decode_paged",
        )(page_table, *args)
    return pl.pallas_call(body, grid=(b, n_steps), in_specs=in_specs, out_specs=out_spec, out_shape=out_shape,
                          scratch_shapes=scratch, compiler_params=cp, name="decode_dense")(*args)


def _nsa_cmp_sample_kernel(pt_ref, q_ref, kpool_ref, vpool_ref, kn_ref, vn_ref, wk0_ref, wv0_ref, slope_ref,
                           e_ref, o_ref, mask_ref, kc_scr, vc_scr, *, n_grp, n_rep, n_pg, nbp, n_sel, pos,
                           scale, ids):
    b = pl.program_id(0)
    d = q_ref.shape[-1]
    gw = n_grp * d
    for p in range(n_pg):
        pid = pt_ref[b, p]
        kc_scr[pl.ds(p, 1), :] = kpool_ref[pl.ds(pid, 1), :]
        vc_scr[pl.ds(p, 1), :] = vpool_ref[pl.ds(pid, 1), :]
    nb = 2 * n_pg + 1
    col = lax.broadcasted_iota(jnp.int32, (1, nbp), 1)
    nid = jnp.where(col < n_pg, 2 * col, jnp.where(col < 2 * n_pg, 2 * (col - n_pg) + 1,
                                                   jnp.where(col == 2 * n_pg, 2 * n_pg, 1 << 20)))
    real = col < nb
    dist = pos - (nid * NSA_BLK + NSA_BLK - 1)
    valid = (dist >= 0) & real
    distf = jnp.where(real, dist, 0).astype(F32)
    cur = pos // NSA_BLK
    forced = (nid == 0) | (nid == cur) | (nid == cur - 1)
    causal = nid * NSA_BLK <= pos
    kn_cmp = jnp.dot(kn_ref[...].astype(BF16), wk0_ref[...], preferred_element_type=F32)
    vn_cmp = jnp.dot(vn_ref[...].astype(BF16), wv0_ref[...], preferred_element_type=F32)
    tail = nbp - 2 * n_pg
    first_row = lax.broadcasted_iota(jnp.int32, (tail, d), 0) == 0
    sels = []
    for g in range(n_grp):
        def blocks(scr, new):
            new_blk = jnp.where(first_row, jnp.broadcast_to(new[g:g + 1], (tail, d)), 0.0)
            return jnp.concatenate([scr[:, g * d:(g + 1) * d], scr[:, gw + g * d:gw + (g + 1) * d],
                                    new_blk], axis=0).astype(BF16)
        kc = blocks(kc_scr, kn_cmp)
        vc = blocks(vc_scr, vn_cmp)
        s = lax.dot_general(q_ref[g], kc, (((1,), (1,)), ((), ())), preferred_element_type=F32)
        s = s * scale - slope_ref[g] * distf
        s = jnp.where(valid, s, NEG_INF)
        m = jnp.max(s, axis=-1, keepdims=True)
        e = jnp.where(valid, jnp.exp(s - m), 0.0)
        p = e / jnp.maximum(jnp.sum(e, axis=-1, keepdims=True), TINY)
        o_ref[g] = jnp.dot(p.astype(BF16), vc, preferred_element_type=F32)
        score = jnp.sum(p, axis=0, keepdims=True)
        score = jnp.where(forced, NSA_FORCED, score)
        score = jnp.where(causal, score, -1.0)
        score = jnp.where(real, score, -2.0)
        sels.append(_rank_select(score, nid, n_sel, ids))
    sel = jnp.concatenate(sels + [jnp.zeros((8 - n_grp, nbp), F32)], axis=0)
    km = jnp.dot(sel.astype(BF16), e_ref[...], preferred_element_type=F32)
    for g in range(n_grp):
        mask_ref[g] = km[g:g + 1]


def nsa_cmp_sample(q, kpool, vpool, kc_new, vc_new, wk0, wv0, slopes, page_table, pos):
    b, n_grp, n_rep, d = q.shape
    n_pg = page_table.shape[1]
    nb = 2 * n_pg + 1
    nbp = _round_up(nb, LANE)
    total = n_pg * PAGE
    ids = [2 * s for s in range(n_pg)] + [2 * s + 1 for s in range(n_pg)] + [2 * n_pg]
    slot_of_block = {bid: s for s, bid in enumerate(ids)}
    key_slot = jnp.asarray([slot_of_block[kp // NSA_BLK] for kp in range(total)]
                           + [slot_of_block[2 * n_pg]] + [-1] * (LANE - 1), jnp.int32)
    expand = (jnp.arange(nbp, dtype=jnp.int32)[:, None] == key_slot[None, :]).astype(BF16)
    npool = kpool.shape[0]
    full = lambda shape: pl.BlockSpec(shape, lambda bb, pt: tuple(0 for _ in shape))
    return pl.pallas_call(
        functools.partial(_nsa_cmp_sample_kernel, n_grp=n_grp, n_rep=n_rep, n_pg=n_pg, nbp=nbp,
                          n_sel=min(NSA_TOPK, nb), pos=pos, scale=d ** -0.5, ids=ids),
        grid_spec=pltpu.PrefetchScalarGridSpec(
            num_scalar_prefetch=1, grid=(b,),
            in_specs=[pl.BlockSpec((None, n_grp, n_rep, d), lambda bb, pt: (bb, 0, 0, 0)),
                      full((npool, 2 * n_grp * d)), full((npool, 2 * n_grp * d)),
                      pl.BlockSpec((None, 8, d), lambda bb, pt: (bb, 0, 0)),
                      pl.BlockSpec((None, 8, d), lambda bb, pt: (bb, 0, 0)),
                      full((d, d)), full((d, d)), full((n_grp, n_rep, 1)), full((nbp, total + LANE))],
            out_specs=[pl.BlockSpec((None, n_grp, n_rep, d), lambda bb, pt: (bb, 0, 0, 0)),
                       pl.BlockSpec((None, n_grp, 1, total + LANE), lambda bb, pt: (bb, 0, 0, 0))],
            scratch_shapes=[pltpu.VMEM((n_pg, 2 * n_grp * d), F32), pltpu.VMEM((n_pg, 2 * n_grp * d), F32)]),
        out_shape=[jax.ShapeDtypeStruct((b, n_grp, n_rep, d), F32),
                   jax.ShapeDtypeStruct((b, n_grp, 1, total + LANE), F32)],
        compiler_params=_cparams(("arbitrary",), 56),
        name="nsa_cmp_sample",
    )(page_table, q, kpool, vpool, kc_new, vc_new, wk0, wv0, slopes.astype(F32).reshape(n_grp, n_rep, 1), expand)


def diff_layer(xp, xs, hp, hs, bsz, tlen, past, cache_k, cache_v, page_table, w_in, lam_vecs, subln, w_o,
               layer_idx):
    lam_init = 0.8 - 0.6 * math.exp(-0.3 * layer_idx)
    slopes = alibi_slopes(A_HEADS, A_KV)
    n_rep = A_HEADS // A_KV
    a_q, a_kw = A_HEADS * 2 * A_HD, A_KV * 2 * A_HD
    w_in_b, w_o_b = w_in.astype(BF16), w_o.astype(BF16)
    diff = (lam_vecs, subln, lam_init)

    proj = matmul(hp, w_in_b).reshape(bsz, tlen, a_q + 2 * a_kw)
    kp = proj[..., a_q:a_q + a_kw].reshape(bsz, tlen, A_KV, 2, A_HD)
    vp = proj[..., a_q + a_kw:].reshape(bsz, tlen, A_KV, 2 * A_HD)
    o = causal_prompt(proj, slopes, n_grp=A_KV, n_rep=n_rep, n_maps=2, d=A_HD, dv=2 * A_HD, q_col=0, k_col=a_q,
                      v_col=a_q + a_kw, diff=diff, out_dtype=BF16)
    xp = matmul(o.reshape(bsz * tlen, a_q), w_o_b, res=xp)

    nb = hs.shape[0]
    npool = cache_k.shape[0]
    proj_s = matmul(hs, w_in_b)
    qs = proj_s[:, :a_q].reshape(nb, A_KV, n_rep, 2, A_HD).transpose(0, 1, 3, 2, 4).astype(BF16)
    ks = proj_s[:, a_q:a_q + a_kw].reshape(nb, 1, A_KV, 2, A_HD)
    vs = proj_s[:, a_q + a_kw:].reshape(nb, 1, A_KV, 2 * A_HD)
    v_rows = cache_v.reshape(npool, PAGE, A_KV, 2, A_HD).transpose(0, 1, 3, 2, 4).reshape(npool, PAGE * 2 * A_KV, A_HD)
    os_ = decode_attn(qs, cache_k.reshape(npool, PAGE * A_KV * 2, A_HD), v_rows,
                      ks.reshape(nb, A_KV, 1, 2 * A_HD), vs.reshape(nb, A_KV, 1, 2 * A_HD),
                      slopes, pos=past, page_table=page_table, pages_per_step=8, diff=diff, out_dtype=BF16)
    xs = matmul(os_.reshape(nb, a_q), w_o_b, res=xs)
    return xp, xs, (kp, ks, vp, vs)


def _cmp_weight(w_cmp):
    blk, d, _ = w_cmp.shape
    eye = jnp.eye(B_KV, dtype=w_cmp.dtype)
    return jnp.einsum('jde,gh->jgdhe', w_cmp, eye).reshape(blk * B_KV * d, B_KV * d).astype(BF16)


def nsa_layer(xp, xs, hp, hs, bsz, tlen, past, cache_kc, cache_vc, cache_ks, cache_vs, buf_kw, buf_vw,
              page_table, w_in, b_gate, w_cmp_k, w_cmp_v, w_o):
    slopes = alibi_slopes(B_HEADS, B_KV)
    n_rep = B_HEADS // B_KV
    b_q, b_kw = B_HEADS * B_HD, B_KV * B_HD
    n_in = w_in.shape[1]
    n_pad = _round_up(n_in, 1024)
    gate_col = b_q + 6 * b_kw
    w_in_b = jnp.pad(w_in, ((0, 0), (0, n_pad - n_in))).astype(BF16)
    w_o_b = w_o.astype(BF16)
    wck, wcv = _cmp_weight(w_cmp_k), _cmp_weight(w_cmp_v)
    kv_col = lambda i: b_q + i * b_kw

    proj = matmul(hp, w_in_b)
    proj3 = proj.reshape(bsz, tlen, n_pad)
    kv = [proj3[..., kv_col(i):kv_col(i + 1)].reshape(bsz, tlen, B_KV, B_HD) for i in range(6)]
    nblk = tlen // NSA_BLK
    cmp_in = lambda a: a.reshape(bsz * nblk, NSA_BLK * b_kw).astype(BF16)
    kc_cmp = matmul(cmp_in(kv[0]), wck).reshape(bsz, nblk, B_KV, B_HD).transpose(0, 2, 1, 3).astype(BF16)
    vc_cmp = matmul(cmp_in(kv[1]), wcv).reshape(bsz, nblk, B_KV, B_HD).transpose(0, 2, 1, 3).astype(BF16)
    o_cmp, sel = nsa_cmp_prompt(proj3, kc_cmp, vc_cmp, slopes, n_grp=B_KV, n_rep=n_rep, d=B_HD)
    expand = (jnp.arange(_round_up(nblk, LANE), dtype=jnp.int32)[:, None]
              == (jnp.arange(tlen, dtype=jnp.int32) // NSA_BLK)[None, :]).astype(BF16)
    o_sel = causal_prompt(proj3, slopes, n_grp=B_KV, n_rep=n_rep, n_maps=1, d=B_HD, dv=B_HD, q_col=0,
                          k_col=kv_col(2), v_col=kv_col(3), sel=(sel, expand))
    o_win = banded_prompt(proj3, proj3, proj3, slopes, window=NSA_WIN, tq=NSA_WIN, n_grp=B_KV, n_rep=n_rep,
                          d=B_HD, dv=B_HD, q_col=0, k_col=kv_col(4), v_col=kv_col(5), rep_chunk=4)
    m = bsz * tlen
    o = nsa_combine(o_cmp.reshape(m, b_q), o_sel.reshape(m, b_q), o_win.reshape(m, b_q), proj, b_gate, gate_col)
    xp = matmul(o, w_o_b, res=xp)

    nb = hs.shape[0]
    proj_s = matmul(hs, w_in_b)
    qs = proj_s[:, :b_q].reshape(nb, B_KV, n_rep, B_HD).astype(BF16)
    kvs = [proj_s[:, kv_col(i):kv_col(i + 1)].reshape(nb, 1, B_KV, B_HD) for i in range(6)]
    npool = cache_kc.shape[0]
    pool_in = lambda c: c.reshape(npool * 2, NSA_BLK * b_kw).astype(BF16)
    kpool = matmul(pool_in(cache_kc), wck).reshape(npool, 2 * b_kw)
    vpool = matmul(pool_in(cache_vc), wcv).reshape(npool, 2 * b_kw)
    new8 = lambda a: jnp.pad(a.reshape(nb, B_KV, B_HD), ((0, 0), (0, 8 - B_KV), (0, 0)))
    o_cmp_s, keymask = nsa_cmp_sample(qs, kpool, vpool, new8(kvs[0]), new8(kvs[1]),
                                      w_cmp_k[0].astype(BF16), w_cmp_v[0].astype(BF16), slopes, page_table, past)
    qs5 = qs[:, :, None]
    new_row = lambda a: a.reshape(nb, B_KV, 1, B_HD)
    head_rows = lambda c: c.reshape(c.shape[0], c.shape[1] * B_KV, B_HD)
    o_sel_s = decode_attn(qs5, head_rows(cache_ks), head_rows(cache_vs), new_row(kvs[2]), new_row(kvs[3]), slopes,
                          pos=past, page_table=page_table, pages_per_step=8, mask=keymask)
    o_win_s = decode_attn(qs5, head_rows(buf_kw), head_rows(buf_vw), new_row(kvs[4]), new_row(kvs[5]), slopes,
                          pos=past, window=NSA_WIN)
    os_ = nsa_combine(o_cmp_s.reshape(nb, b_q), o_sel_s.reshape(nb, b_q), o_win_s.reshape(nb, b_q), proj_s,
                      b_gate, gate_col)
    xs = matmul(os_, w_o_b, res=xs)

    wb = buf_kw.shape[1]
    wp = min(NSA_WIN, tlen)
    kw_all = jnp.concatenate([buf_kw, kvs[4]], axis=1)
    vw_all = jnp.concatenate([buf_vw, kvs[5]], axis=1)
    state = (kv[0], kvs[0], kv[1], kvs[1], kv[2], kvs[2], kv[3], kvs[3],
             kv[4][:, tlen - wp:], kw_all[:, -wb:], kv[5][:, tlen - wp:], vw_all[:, -wb:])
    return xp, xs, state


def swa_layer(xp, xs, hp, hs, bsz, tlen, past, buf_k, buf_v, w_in, b_in, sinks, w_o, b_o):
    slopes = alibi_slopes(C_HEADS, C_KV)
    sink = sinks.astype(F32).reshape(C_KV, C_HEADS // C_KV)
    n_rep = C_HEADS // C_KV
    c_q, c_kw = C_HEADS * C_HD, C_KV * C_HD
    w_in_b, w_o_b = w_in.astype(BF16), w_o.astype(BF16)
    heads = lambda a: a.transpose(0, 2, 1, 3).astype(BF16)

    proj = matmul(hp, w_in_b, bias=b_in).reshape(bsz, tlen, c_q + 2 * c_kw)
    q = proj[..., :c_q].reshape(bsz, tlen, C_KV, n_rep, C_HD).transpose(0, 2, 3, 1, 4).astype(BF16)
    kp = proj[..., c_q:c_q + c_kw].reshape(bsz, tlen, C_KV, C_HD)
    vp = proj[..., c_q + c_kw:].reshape(bsz, tlen, C_KV, C_HD)
    o = banded_prompt(q, heads(kp), heads(vp), slopes, window=C_WIN, tq=C_WIN, n_grp=C_KV, n_rep=n_rep, d=C_HD,
                      dv=C_HD, sink=sink, out_dtype=BF16)
    o = o.transpose(0, 3, 1, 2, 4).reshape(bsz * tlen, c_q)
    xp = matmul(o, w_o_b, bias=b_o, res=xp)

    nb = hs.shape[0]
    proj_s = matmul(hs, w_in_b, bias=b_in)
    qs = proj_s[:, :c_q].reshape(nb, C_KV, 1, n_rep, C_HD).astype(BF16)
    ks = proj_s[:, c_q:c_q + c_kw].reshape(nb, 1, C_KV, C_HD)
    vs = proj_s[:, c_q + c_kw:].reshape(nb, 1, C_KV, C_HD)
    head_rows = lambda c: c.reshape(c.shape[0], c.shape[1] * C_KV, C_HD)
    os_ = decode_attn(qs, head_rows(buf_k), head_rows(buf_v), ks.reshape(nb, C_KV, 1, C_HD),
                      vs.reshape(nb, C_KV, 1, C_HD), slopes, pos=past, window=C_WIN, sink=sink, out_dtype=BF16)
    xs = matmul(os_.reshape(nb, c_q), w_o_b, bias=b_o, res=xs)

    wb = buf_k.shape[1]
    wp = min(C_WIN, tlen)
    k_all = jnp.concatenate([buf_k, ks], axis=1)
    v_all = jnp.concatenate([buf_v, vs], axis=1)
    return xp, xs, (kp[:, tlen - wp:], k_all[:, -wb:], vp[:, tlen - wp:], v_all[:, -wb:])


def kernel(x_prompt, x_sample, cache_l0_k, cache_l0_v, cache_l1_kc, cache_l1_vc, cache_l1_ks, cache_l1_vs, state_l1_kw, state_l1_vw, state_l2_k, state_l2_v, cache_l3_k, cache_l3_v, page_table, norm_ffa, norm_mix, norm_ffb, norm_final, ffa_w13, ffa_w2, ffb_w13, ffb_w2, l0_w_in, l0_lam, l0_subln, l0_w_o, l1_w_in, l1_b_gate, l1_w_cmp_k, l1_w_cmp_v, l1_w_o, l2_w_in, l2_b_in, l2_sinks, l2_w_o, l2_b_o, l3_w_in, l3_lam, l3_subln, l3_w_o):
    bsz, tlen, dm = x_prompt.shape
    nb, dec_seq, _ = x_sample.shape
    assert dec_seq == 1 and dm == D_MODEL
    past = page_table.shape[1] * PAGE
    xp = x_prompt.reshape(bsz * tlen, dm)
    xs = x_sample.reshape(nb, dm)
    f = ffa_w2.shape[1]
    f_pad = _round_up(f, 512)
    pad_rows = lambda w: jnp.pad(w.astype(BF16), ((0, 0), (0, f_pad - f), (0, 0)))
    ffa13, ffb13 = ffa_w13.astype(BF16), ffb_w13.astype(BF16)
    ffa2, ffb2 = pad_rows(ffa_w2), pad_rows(ffb_w2)
    a_params = {0: (cache_l0_k, cache_l0_v, l0_w_in, l0_lam, l0_subln, l0_w_o),
                3: (cache_l3_k, cache_l3_v, l3_w_in, l3_lam, l3_subln, l3_w_o)}
    state = {}
    for i in range(4):
        xp, xs = ffn_block(xp, xs, norm_ffa[i], ffa13, ffa2, i)
        hp = rmsnorm(xp, norm_mix[i], BF16)
        hs = rmsnorm(xs, norm_mix[i], BF16)
        kind = i % 3
        if kind == 0:
            ck, cv, w_in, lam, subln, w_o = a_params[i]
            xp, xs, st = diff_layer(xp, xs, hp, hs, bsz, tlen, past, ck, cv, page_table, w_in, lam, subln, w_o, i)
        elif kind == 1:
            xp, xs, st = nsa_layer(xp, xs, hp, hs, bsz, tlen, past, cache_l1_kc, cache_l1_vc, cache_l1_ks,
                                   cache_l1_vs, state_l1_kw, state_l1_vw, page_table, l1_w_in, l1_b_gate,
                                   l1_w_cmp_k, l1_w_cmp_v, l1_w_o)
        else:
            xp, xs, st = swa_layer(xp, xs, hp, hs, bsz, tlen, past, state_l2_k, state_l2_v, l2_w_in, l2_b_in,
                                   l2_sinks, l2_w_o, l2_b_o)
        state[i] = st
        xp, xs = ffn_block(xp, xs, norm_ffb[i], ffb13, ffb2, i)
    y_prompt = rmsnorm(xp, norm_final, F32).reshape(bsz, tlen, dm)
    y_sample = rmsnorm(xs, norm_final, F32).reshape(nb, 1, dm)
    return (y_prompt, y_sample) + tuple(state[0]) + tuple(state[1]) + tuple(state[2]) + tuple(state[3])
```

```python
import functools
import math

import jax
import jax.numpy as jnp
from jax import lax
from jax.experimental import pallas as pl
from jax.experimental.pallas import tpu as pltpu

NORM_EPS = 1e-6
NEG_INF = -1e30
TINY = 1e-30

D_MODEL = 4096
PAGE = 128
A_HD, A_HEADS, A_KV = 128, 16, 4
B_HD, B_HEADS, B_KV = 128, 32, 2
NSA_BLK, NSA_TOPK, NSA_WIN, NSA_FORCED = 64, 16, 512, 1e4
C_HD, C_HEADS, C_KV, C_WIN = 64, 64, 8, 128

LANE = 128
QT = 256
KT = 512
FFN_TN = 256
FULL_K_TILE_BYTES = 12 * 1024 * 1024
MIB = 1024 * 1024
BF16 = jnp.bfloat16
F32 = jnp.float32


def _cparams(sem, vmem_mib):
    return pltpu.CompilerParams(dimension_semantics=sem, vmem_limit_bytes=vmem_mib * MIB)


def _round_up(x, m):
    return -(-x // m) * m


def _pick(n, prefs):
    for p in prefs:
        if n % p == 0:
            return p
    return n


def _rep(x, width):
    return x if width == LANE else jnp.concatenate([x] * (width // LANE), axis=1)


def _rms_kernel(x_ref, g_ref, o_ref):
    x = x_ref[...]
    ms = jnp.mean(x * x, axis=-1, keepdims=True)
    o_ref[...] = (x * lax.rsqrt(ms + NORM_EPS) * g_ref[...]).astype(o_ref.dtype)


def rmsnorm(x, g, out_dtype):
    m, d = x.shape
    tm = _pick(m, (256, 128, 64, 32, 16, 8))
    return pl.pallas_call(
        _rms_kernel,
        grid=(m // tm,),
        in_specs=[pl.BlockSpec((tm, d), lambda i: (i, 0)), pl.BlockSpec((1, d), lambda i: (0, 0))],
        out_specs=pl.BlockSpec((tm, d), lambda i: (i, 0)),
        out_shape=jax.ShapeDtypeStruct((m, d), out_dtype),
        compiler_params=_cparams(("parallel",), 40),
        name="rmsnorm",
    )(x, g.reshape(1, d).astype(F32))


def _mm_kernel(*refs, nk, has_bias, has_res, res_scale):
    it = iter(refs)
    a_ref, w_ref = next(it), next(it)
    b_ref = next(it) if has_bias else None
    r_ref = next(it) if has_res else None
    o_ref = next(it)
    acc_ref = next(it) if nk > 1 else None

    def epilogue(acc):
        if has_bias:
            acc = acc + b_ref[...]
        if has_res:
            acc = r_ref[...] + res_scale * acc
        o_ref[...] = acc.astype(o_ref.dtype)

    d = jnp.dot(a_ref[...], w_ref[...], preferred_element_type=F32)
    if nk == 1:
        epilogue(d)
    else:
        k = pl.program_id(2)

        @pl.when(k == 0)
        def _():
            acc_ref[...] = d

        @pl.when(k > 0)
        def _():
            acc_ref[...] += d

        @pl.when(k == nk - 1)
        def _():
            epilogue(acc_ref[...])


def matmul(a, w, *, layer=None, bias=None, res=None, res_scale=1.0, out_dtype=F32):
    m, k = a.shape
    n = w.shape[-1]
    tn = _pick(n, (512, 256, 128))
    if k <= 4096:
        tm, tk = _pick(m, (1024, 512, 256, 128, 64, 32, 16, 8)), k
    elif k * tn * 2 <= FULL_K_TILE_BYTES:
        tm, tk = _pick(m, (512, 256, 128, 64, 32, 16, 8)), k
    else:
        tm, tk = _pick(m, (1024, 512, 256, 128, 64, 32, 16, 8)), _pick(k, (2048, 1024, 512, 256, 128))
    nk = k // tk
    if w.ndim == 3:
        w_spec = pl.BlockSpec((None, tk, tn), lambda i, j, kk: (layer, kk, j))
    else:
        w_spec = pl.BlockSpec((tk, tn), lambda i, j, kk: (kk, j))
    in_specs = [pl.BlockSpec((tm, tk), lambda i, j, kk: (i, kk)), w_spec]
    args = [a, w]
    if bias is not None:
        in_specs.append(pl.BlockSpec((1, tn), lambda i, j, kk: (0, j)))
        args.append(bias.reshape(1, n).astype(F32))
    if res is not None:
        in_specs.append(pl.BlockSpec((tm, tn), lambda i, j, kk: (i, j)))
        args.append(res)
    return pl.pallas_call(
        functools.partial(_mm_kernel, nk=nk, has_bias=bias is not None, has_res=res is not None,
                          res_scale=res_scale),
        grid=(m // tm, n // tn, nk),
        in_specs=in_specs,
        out_specs=pl.BlockSpec((tm, tn), lambda i, j, kk: (i, j)),
        out_shape=jax.ShapeDtypeStruct((m, n), out_dtype),
        scratch_shapes=[pltpu.VMEM((tm, tn), F32)] if nk > 1 else [],
        compiler_params=_cparams(("parallel", "parallel", "arbitrary"), 56),
        name="matmul",
    )(*args)


def _ffn1_kernel(a_ref, wg_ref, wu_ref, o_ref):
    a = a_ref[...]
    g = jnp.dot(a, wg_ref[...], preferred_element_type=F32)
    u = jnp.dot(a, wu_ref[...], preferred_element_type=F32)
    o_ref[...] = (g / (1.0 + jnp.exp(-g)) * u).astype(o_ref.dtype)


def ffn1(a, w13, layer):
    m, k = a.shape
    f = w13.shape[-1] // 2
    tn = _pick(f, (FFN_TN, LANE))
    n_tiles = f // tn
    tm = _pick(m, (2048, 1024, 512, 256, 128, 64, 32, 16, 8))
    return pl.pallas_call(
        _ffn1_kernel,
        grid=(m // tm, n_tiles),
        in_specs=[pl.BlockSpec((tm, k), lambda i, j: (i, 0)),
                  pl.BlockSpec((None, k, tn), lambda i, j: (layer, 0, j)),
                  pl.BlockSpec((None, k, tn), lambda i, j: (layer, 0, j + n_tiles))],
        out_specs=pl.BlockSpec((tm, tn), lambda i, j: (i, j)),
        out_shape=jax.ShapeDtypeStruct((m, f), BF16),
        compiler_params=_cparams(("parallel", "parallel"), 56),
        name="ffn1",
    )(a, w13, w13)


def ffn_block(xp, xs, g, w13, w2, layer):
    outs = []
    for x in (xp, xs):
        h = rmsnorm(x, g, BF16)
        mid = ffn1(h, w13, layer)
        outs.append(matmul(mid, w2, layer=layer, res=x, res_scale=0.5))
    return outs


def alibi_slopes(n_heads, n_groups):
    i = jnp.arange(1, n_heads + 1, dtype=F32)
    return jnp.exp2(-8.0 * i / n_heads).reshape(n_groups, n_heads // n_groups)


def _diff_finish(o0, o1, lam_ref, subln_ref, lam_init):
    lv = lam_ref[...]
    lam = (jnp.exp(jnp.sum(lv[0:1] * lv[1:2], axis=-1, keepdims=True))
           - jnp.exp(jnp.sum(lv[2:3] * lv[3:4], axis=-1, keepdims=True)) + lam_init)
    o = o0 - lam * o1
    ms = jnp.mean(o * o, axis=-1, keepdims=True)
    return o * lax.rsqrt(ms + NORM_EPS) * subln_ref[...] * (1.0 - lam_init)


def _causal_kernel(*refs, n_maps, n_rep, d, dv, nkt, scale, has_sel, diff, lam_init):
    it = iter(refs)
    slope_ref = next(it)
    q_ref, k_ref, v_ref = next(it), next(it), next(it)
    sel_ref, e_ref = (next(it), next(it)) if has_sel else (None, None)
    lam_ref, subln_ref = (next(it), next(it)) if diff else (None, None)
    o_ref = next(it)
    q_scr, m_ref, l_ref, acc_ref, e_scr, a_scr = (next(it) for _ in range(6))

    tq, tk = QT, KT
    ratio = tk // tq
    g = pl.program_id(1)
    i = pl.program_id(2)
    kt = pl.program_id(3)
    diag = i // ratio

    @pl.when(kt == 0)
    def _():
        m_ref[...] = jnp.full(m_ref.shape, NEG_INF, F32)
        l_ref[...] = jnp.zeros(l_ref.shape, F32)
        acc_ref[...] = jnp.zeros(acc_ref.shape, F32)
        for c in range(n_maps):
            for r in range(n_rep):
                col = (r * n_maps + c) * d
                q_scr[c, r * tq:(r + 1) * tq, :] = q_ref[:, col:col + d].astype(BF16)

    def step(causal):
        kposf = (kt * tk - i * tq + lax.broadcasted_iota(jnp.int32, (1, tk), 1)).astype(F32)
        mask = None
        if causal:
            mask = (kt * tk + lax.broadcasted_iota(jnp.int32, (tq, tk), 1)
                    <= i * tq + lax.broadcasted_iota(jnp.int32, (tq, tk), 0))
        if has_sel:
            picked = jnp.dot(sel_ref[...].astype(BF16), e_ref[...], preferred_element_type=F32) > 0.5
            mask = picked if mask is None else mask & picked
        kb = k_ref[...].astype(BF16)
        vb = v_ref[...].astype(BF16)
        for c in range(n_maps):
            s_all = lax.dot_general(q_scr[c], kb[:, c * d:(c + 1) * d], (((1,), (1,)), ((), ())),
                                    preferred_element_type=F32)
            for r in range(n_rep):
                sl = slice(r * tq, (r + 1) * tq)
                s = s_all[sl] * scale + slope_ref[g * n_rep + r] * kposf
                if mask is not None:
                    s = jnp.where(mask, s, NEG_INF)
                m_old = m_ref[c, sl]
                m_new = jnp.maximum(m_old, jnp.max(s, axis=-1, keepdims=True))
                e = jnp.exp(s - _rep(m_new, tk))
                if mask is not None:
                    e = jnp.where(mask, e, 0.0)
                alpha = jnp.exp(m_old - m_new)
                l_ref[c, sl] = alpha * l_ref[c, sl] + jnp.sum(e, axis=-1, keepdims=True)
                m_ref[c, sl] = m_new
                a_scr[sl] = alpha
                e_scr[sl] = e.astype(BF16)
            acc_ref[c] = _rep(a_scr[...], dv) * acc_ref[c] + jnp.dot(e_scr[...], vb, preferred_element_type=F32)

    @pl.when(kt < diag)
    def _():
        step(False)

    @pl.when(kt == diag)
    def _():
        step(True)

    @pl.when(kt == nkt - 1)
    def _():
        o = acc_ref[0] / jnp.maximum(_rep(l_ref[0], dv), TINY)
        if diff:
            o = _diff_finish(o, acc_ref[1] / jnp.maximum(_rep(l_ref[1], dv), TINY), lam_ref, subln_ref, lam_init)
        for r in range(n_rep):
            o_ref[:, r * dv:(r + 1) * dv] = o[r * tq:(r + 1) * tq].astype(o_ref.dtype)


def causal_prompt(proj, slopes, *, n_grp, n_rep, n_maps, d, dv, q_col, k_col, v_col, sel=None, diff=None,
                  out_dtype=F32):
    b, tlen, _ = proj.shape
    tq, tk = QT, KT
    ratio = tk // tq
    nq, nkt = tlen // tq, tlen // tk
    rows = n_rep * tq
    qw, kw = n_rep * n_maps * d, n_maps * d
    kidx = lambda i, kt: jnp.minimum(kt, i // ratio)
    in_specs = [
        pl.BlockSpec(memory_space=pltpu.SMEM),
        pl.BlockSpec((None, tq, qw), lambda bb, gg, i, kt: (bb, i, q_col // qw + gg)),
        pl.BlockSpec((None, tk, kw), lambda bb, gg, i, kt: (bb, kidx(i, kt), k_col // kw + gg)),
        pl.BlockSpec((None, tk, dv), lambda bb, gg, i, kt: (bb, kidx(i, kt), v_col // dv + gg)),
    ]
    args = [slopes.reshape(-1).astype(F32), proj, proj, proj]
    if sel is not None:
        sel_mask, expand = sel
        nbp = sel_mask.shape[-1]
        in_specs.append(pl.BlockSpec((None, None, tq, nbp), lambda bb, gg, i, kt: (bb, gg, i, 0)))
        in_specs.append(pl.BlockSpec((nbp, tk), lambda bb, gg, i, kt: (0, kidx(i, kt))))
        args += [sel_mask, expand]
    lam_init = 0.0
    if diff is not None:
        lam_vecs, subln, lam_init = diff
        in_specs.append(pl.BlockSpec((4, A_HD), lambda bb, gg, i, kt: (0, 0)))
        in_specs.append(pl.BlockSpec((1, dv), lambda bb, gg, i, kt: (0, 0)))
        args += [lam_vecs.astype(F32), subln.reshape(1, dv).astype(F32)]
    return pl.pallas_call(
        functools.partial(_causal_kernel, n_maps=n_maps, n_rep=n_rep, d=d, dv=dv, nkt=nkt, scale=d ** -0.5,
                          has_sel=sel is not None, diff=diff is not None, lam_init=lam_init),
        grid=(b, n_grp, nq, nkt),
        in_specs=in_specs,
        out_specs=pl.BlockSpec((None, tq, n_rep * dv), lambda bb, gg, i, kt: (bb, i, gg)),
        out_shape=jax.ShapeDtypeStruct((b, tlen, n_grp * n_rep * dv), out_dtype),
        scratch_shapes=[pltpu.VMEM((n_maps, rows, d), BF16), pltpu.VMEM((n_maps, rows, LANE), F32),
                        pltpu.VMEM((n_maps, rows, LANE), F32), pltpu.VMEM((n_maps, rows, dv), F32),
                        pltpu.VMEM((rows, tk), BF16), pltpu.VMEM((rows, LANE), F32)],
        compiler_params=_cparams(("parallel", "parallel", "parallel", "arbitrary"), 48),
        name="causal_prompt",
    )(*args)


def _banded_kernel(*refs, n_rep, n_chunks, tq, d, dv, window, scale, has_sink, head_major):
    it = iter(refs)
    slope_ref = next(it)
    sink_ref = next(it) if has_sink else None
    q_ref, kp_ref, kc_ref, vp_ref, vc_ref, o_ref = (next(it) for _ in range(6))
    q_scr, e_scr = next(it), next(it)

    tk = 2 * tq
    if head_major:
        g, i = pl.program_id(1), pl.program_id(2)
        head0 = g * n_rep
    else:
        i = pl.program_id(2)
        head0 = pl.program_id(1) * n_rep
    for r in range(n_rep):
        if head_major:
            q_scr[r * tq:(r + 1) * tq, :] = q_ref[r]
        else:
            q_scr[r * tq:(r + 1) * tq, :] = q_ref[:, r * d:(r + 1) * d].astype(BF16)
    kb = jnp.concatenate([kp_ref[...], kc_ref[...]], axis=0).astype(BF16)
    vb = jnp.concatenate([vp_ref[...], vc_ref[...]], axis=0).astype(BF16)
    s_all = lax.dot_general(q_scr[...], kb, (((1,), (1,)), ((), ())), preferred_element_type=F32)
    row = lax.broadcasted_iota(jnp.int32, (tq, tk), 0)
    col = lax.broadcasted_iota(jnp.int32, (tq, tk), 1)
    dist = row + tq - col
    mask = (dist >= 0) & (dist <= window) & ((i - 1) * tq + col >= 0)
    kposf = (lax.broadcasted_iota(jnp.int32, (1, tk), 1) - tq).astype(F32)
    rowf = lax.broadcasted_iota(jnp.int32, (tq, 1), 0).astype(F32)
    for r in range(n_rep):
        sl = slice(r * tq, (r + 1) * tq)
        slope = slope_ref[head0 + r]
        s = s_all[sl] * scale + slope * kposf
        s = jnp.where(mask, s, NEG_INF)
        m = jnp.max(s, axis=-1, keepdims=True)
        if has_sink:
            sink = sink_ref[head0 + r] + slope * rowf
            m = jnp.maximum(m, sink)
        e = jnp.where(mask, jnp.exp(s - m), 0.0)
        den = jnp.sum(e, axis=-1, keepdims=True)
        if has_sink:
            den = den + jnp.exp(sink - m)
        e_scr[sl] = (e / jnp.maximum(den, TINY)).astype(BF16)
    o = jnp.dot(e_scr[...], vb, preferred_element_type=F32)
    for r in range(n_rep):
        if head_major:
            o_ref[r] = o[r * tq:(r + 1) * tq].astype(o_ref.dtype)
        else:
            o_ref[:, r * dv:(r + 1) * dv] = o[r * tq:(r + 1) * tq].astype(o_ref.dtype)


def banded_prompt(q, k, v, slopes, *, window, tq, n_grp, n_rep, d, dv, q_col=0, k_col=0, v_col=0, rep_chunk=None,
                  sink=None, out_dtype=F32):
    head_major = rep_chunk is None
    assert window <= tq
    tlen = q.shape[3] if head_major else q.shape[1]
    b = q.shape[0]
    nq = tlen // tq
    prev = lambda i: jnp.maximum(i - 1, 0)
    smem = pl.BlockSpec(memory_space=pltpu.SMEM)
    in_specs, args = [smem], [slopes.reshape(-1).astype(F32)]
    if sink is not None:
        in_specs.append(smem)
        args.append(sink.reshape(-1).astype(F32))
    if head_major:
        rc, n_chunks = n_rep, 1
        grid = (b, n_grp, nq)
        in_specs += [pl.BlockSpec((None, None, rc, tq, d), lambda bb, gg, i: (bb, gg, 0, i, 0)),
                     pl.BlockSpec((None, None, tq, d), lambda bb, gg, i: (bb, gg, prev(i), 0)),
                     pl.BlockSpec((None, None, tq, d), lambda bb, gg, i: (bb, gg, i, 0)),
                     pl.BlockSpec((None, None, tq, dv), lambda bb, gg, i: (bb, gg, prev(i), 0)),
                     pl.BlockSpec((None, None, tq, dv), lambda bb, gg, i: (bb, gg, i, 0))]
        out_spec = pl.BlockSpec((None, None, rc, tq, dv), lambda bb, gg, i: (bb, gg, 0, i, 0))
        out_shape = jax.ShapeDtypeStruct((b, n_grp, n_rep, tlen, dv), out_dtype)
    else:
        rc = rep_chunk
        n_chunks = n_rep // rc
        grid = (b, n_grp * n_chunks, nq)
        kcol = lambda gc: k_col // d + gc // n_chunks
        vcol = lambda gc: v_col // dv + gc // n_chunks
        in_specs += [pl.BlockSpec((None, tq, rc * d), lambda bb, gc, i: (bb, i, q_col // (rc * d) + gc)),
                     pl.BlockSpec((None, tq, d), lambda bb, gc, i: (bb, prev(i), kcol(gc))),
                     pl.BlockSpec((None, tq, d), lambda bb, gc, i: (bb, i, kcol(gc))),
                     pl.BlockSpec((None, tq, dv), lambda bb, gc, i: (bb, prev(i), vcol(gc))),
                     pl.BlockSpec((None, tq, dv), lambda bb, gc, i: (bb, i, vcol(gc)))]
        out_spec = pl.BlockSpec((None, tq, rc * dv), lambda bb, gc, i: (bb, i, gc))
        out_shape = jax.ShapeDtypeStruct((b, tlen, n_grp * n_rep * dv), out_dtype)
    args += [q, k, k, v, v]
    return pl.pallas_call(
        functools.partial(_banded_kernel, n_rep=rc, n_chunks=n_chunks, tq=tq, d=d, dv=dv, window=window,
                          scale=d ** -0.5, has_sink=sink is not None, head_major=head_major),
        grid=grid,
        in_specs=in_specs,
        out_specs=out_spec,
        out_shape=out_shape,
        scratch_shapes=[pltpu.VMEM((rc * tq, d), BF16), pltpu.VMEM((rc * tq, 2 * tq), BF16)],
        compiler_params=_cparams(("parallel", "parallel", "parallel"), 56),
        name="banded_prompt",
    )(*args)


def _rank_select(score, block_id, n_sel, ids):
    rank = jnp.zeros(score.shape, F32)
    for m, bid in enumerate(ids):
        cm = score[:, m:m + 1]
        beats = (cm > score) | ((cm == score) & (bid < block_id))
        rank = rank + jnp.where(beats, 1.0, 0.0)
    return jnp.where(rank < n_sel, 1.0, 0.0)


def _nsa_cmp_prompt_kernel(slope_ref, q_ref, kc_ref, vc_ref, o_ref, sel_ref, q_scr, *, n_rep, nb, n_sel, scale):
    t = QT
    d = kc_ref.shape[-1]
    nbp = kc_ref.shape[0]
    g = pl.program_id(1)
    i = pl.program_id(2)
    for r in range(n_rep):
        q_scr[r * t:(r + 1) * t, :] = q_ref[:, r * d:(r + 1) * d].astype(BF16)
    s_all = lax.dot_general(q_scr[...], kc_ref[...], (((1,), (1,)), ((), ())), preferred_element_type=F32)
    vc = vc_ref[...]
    pos = i * t + lax.broadcasted_iota(jnp.int32, (t, nbp), 0)
    blk = lax.broadcasted_iota(jnp.int32, (t, nbp), 1)
    real = blk < nb
    dist = pos - (blk * NSA_BLK + NSA_BLK - 1)
    distf = jnp.where(real, dist, 0).astype(F32)
    valid = (dist >= 0) & real
    score = jnp.zeros((t, nbp), F32)
    for r in range(n_rep):
        s = s_all[r * t:(r + 1) * t] * scale - slope_ref[g * n_rep + r] * distf
        s = jnp.where(valid, s, NEG_INF)
        m = jnp.max(s, axis=-1, keepdims=True)
        e = jnp.where(valid, jnp.exp(s - m), 0.0)
        p = e / jnp.maximum(jnp.sum(e, axis=-1, keepdims=True), TINY)
        score = score + p
        o_ref[:, r * d:(r + 1) * d] = jnp.dot(p.astype(BF16), vc, preferred_element_type=F32)
    cur = pos // NSA_BLK
    forced = (blk == 0) | (blk == cur) | (blk == cur - 1)
    causal = blk * NSA_BLK <= pos
    score = jnp.where(forced, NSA_FORCED, score)
    score = jnp.where(causal, score, -1.0)
    score = jnp.where(real, score, -2.0)
    sel_ref[...] = _rank_select(score, blk, n_sel, range(nb))


def nsa_cmp_prompt(proj, kc_cmp, vc_cmp, slopes, *, n_grp, n_rep, d):
    b, tlen, _ = proj.shape
    nb = kc_cmp.shape[2]
    nbp = _round_up(nb, LANE)
    kc_cmp = jnp.pad(kc_cmp, ((0, 0), (0, 0), (0, nbp - nb), (0, 0)))
    vc_cmp = jnp.pad(vc_cmp, ((0, 0), (0, 0), (0, nbp - nb), (0, 0)))
    t = QT
    return pl.pallas_call(
        functools.partial(_nsa_cmp_prompt_kernel, n_rep=n_rep, nb=nb, n_sel=min(NSA_TOPK, nb), scale=d ** -0.5),
        grid=(b, n_grp, tlen // t),
        in_specs=[pl.BlockSpec(memory_space=pltpu.SMEM),
                  pl.BlockSpec((None, t, n_rep * d), lambda bb, gg, i: (bb, i, gg)),
                  pl.BlockSpec((None, None, nbp, d), lambda bb, gg, i: (bb, gg, 0, 0)),
                  pl.BlockSpec((None, None, nbp, d), lambda bb, gg, i: (bb, gg, 0, 0))],
        out_specs=[pl.BlockSpec((None, t, n_rep * d), lambda bb, gg, i: (bb, i, gg)),
                   pl.BlockSpec((None, None, t, nbp), lambda bb, gg, i: (bb, gg, i, 0))],
        out_shape=[jax.ShapeDtypeStruct((b, tlen, n_grp * n_rep * d), F32),
                   jax.ShapeDtypeStruct((b, n_grp, tlen, nbp), F32)],
        scratch_shapes=[pltpu.VMEM((n_rep * t, d), BF16)],
        compiler_params=_cparams(("parallel", "parallel", "parallel"), 48),
        name="nsa_cmp_prompt",
    )(slopes.reshape(-1).astype(F32), proj, kc_cmp, vc_cmp)


def _nsa_combine_kernel(oc_ref, os_ref, ow_ref, gl_ref, bg_ref, o_ref, *, n_heads, d):
    gates = 1.0 / (1.0 + jnp.exp(-(gl_ref[...] + bg_ref[...])))
    for h in range(n_heads):
        sl = slice(h * d, (h + 1) * d)
        o = (oc_ref[:, sl] * gates[:, 3 * h:3 * h + 1] + os_ref[:, sl] * gates[:, 3 * h + 1:3 * h + 2]
             + ow_ref[:, sl] * gates[:, 3 * h + 2:3 * h + 3])
        o_ref[:, sl] = o.astype(o_ref.dtype)


def nsa_combine(o_cmp, o_sel, o_win, proj, b_gate, gate_col):
    m, n = o_cmp.shape
    tm = _pick(m, (256, 128, 64, 32, 16, 8))
    blk = lambda: pl.BlockSpec((tm, n), lambda i: (i, 0))
    bg = jnp.pad(b_gate.astype(F32), (0, LANE - b_gate.shape[0])).reshape(1, LANE)
    return pl.pallas_call(
        functools.partial(_nsa_combine_kernel, n_heads=B_HEADS, d=B_HD),
        grid=(m // tm,),
        in_specs=[blk(), blk(), blk(),
                  pl.BlockSpec((tm, LANE), lambda i: (i, gate_col // LANE)),
                  pl.BlockSpec((1, LANE), lambda i: (0, 0))],
        out_specs=blk(),
        out_shape=jax.ShapeDtypeStruct((m, n), BF16),
        compiler_params=_cparams(("parallel",), 48),
        name="nsa_combine",
    )(o_cmp, o_sel, o_win, proj, bg)


def _decode_kernel(*refs, n_grp, n_maps, n_rep, dv, n_pages, n_steps, paged, chunk, kpos0, pos, window,
                   scale, has_mask, has_sink, diff, lam_init):
    it = iter(refs)
    if paged:
        next(it)
    q_ref = next(it)
    k_refs = [next(it) for _ in range(n_pages)]
    v_refs = [next(it) for _ in range(n_pages)]
    kn_ref, vn_ref, slope_ref = next(it), next(it), next(it)
    sink_ref = next(it) if has_sink else None
    mask_ref, maskn_ref = (next(it), next(it)) if has_mask else (None, None)
    lam_ref, subln_ref = (next(it), next(it)) if diff else (None, None)
    o_ref = next(it)
    m_ref, l_ref, acc_ref = next(it), next(it), next(it)

    rows = n_maps * n_rep
    hk = n_grp * n_maps
    vh = max(dv // LANE, 1)
    hv = vh * n_grp
    npos = chunk // n_pages
    step = pl.program_id(1)

    @pl.when(step == 0)
    def _():
        if has_sink:
            m_ref[...] = sink_ref[...]
            l_ref[...] = jnp.ones(l_ref.shape, F32)
        else:
            m_ref[...] = jnp.full(m_ref.shape, NEG_INF, F32)
            l_ref[...] = jnp.zeros(l_ref.shape, F32)
        acc_ref[...] = jnp.zeros(acc_ref.shape, F32)

    kpos = kpos0 + step * chunk + lax.broadcasted_iota(jnp.int32, (1, chunk), 1)
    dist = pos - kpos
    ok = (dist >= 0) & (kpos >= 0)
    if window is not None:
        ok = ok & (dist <= window)
    distf = dist.astype(F32)
    for g in range(n_grp):
        okf = jnp.where(ok, 1.0, 0.0)
        if has_mask:
            okf = okf * mask_ref[g]
        okg = jnp.broadcast_to(okf, (rows, chunk)) > 0.5
        kcat = jnp.concatenate(
            [jnp.concatenate([r[pl.ds(g * n_maps + c, npos, stride=hk), :] for c in range(n_maps)], axis=1)
             for r in k_refs], axis=0).astype(BF16)
        vcat = jnp.concatenate(
            [jnp.concatenate([r[pl.ds(h * n_grp + g, npos, stride=hv), :] for h in range(vh)], axis=1)
             for r in v_refs], axis=0).astype(BF16)
        s = lax.dot_general(q_ref[g], kcat, (((1,), (1,)), ((), ())), preferred_element_type=F32)
        s = s * scale - slope_ref[g] * distf
        s = jnp.where(okg, s, NEG_INF)
        m_old = m_ref[g]
        m_new = jnp.maximum(m_old, jnp.max(s, axis=-1, keepdims=True))
        alpha = jnp.exp(m_old - m_new)
        e = jnp.where(okg, jnp.exp(s - m_new), 0.0)
        l_ref[g] = alpha * l_ref[g] + jnp.sum(e, axis=-1, keepdims=True)
        acc_ref[g] = alpha * acc_ref[g] + jnp.dot(e.astype(BF16), vcat, preferred_element_type=F32)
        m_ref[g] = m_new

    @pl.when(step == n_steps - 1)
    def _():
        for g in range(n_grp):
            kn = kn_ref[g].astype(BF16).astype(F32)
            vn = vn_ref[g].astype(BF16).astype(F32)
            s = jnp.sum(q_ref[g].astype(F32) * kn, axis=-1, keepdims=True) * scale
            if has_mask:
                okn = maskn_ref[g][:, 0:1] > 0.5
                s = jnp.where(okn, s, NEG_INF)
            m_old = m_ref[g]
            m_new = jnp.maximum(m_old, s)
            alpha = jnp.exp(m_old - m_new)
            e = jnp.exp(s - m_new)
            if has_mask:
                e = jnp.where(okn, e, 0.0)
            den = alpha * l_ref[g] + e
            o = (alpha * acc_ref[g] + e * vn) / jnp.maximum(den, TINY)
            if diff:
                o = _diff_finish(o[0:n_rep], o[n_rep:rows], lam_ref, subln_ref, lam_init)
            o_ref[g] = o.astype(o_ref.dtype)


def _block_diag_q(q):
    b, g, c, r, d = q.shape
    eye = jnp.eye(c, dtype=q.dtype)
    return jnp.einsum('bgcrd,ce->bgcred', q, eye).reshape(b, g, c * r, c * d)


def decode_attn(q, k_main, v_main, k_new, v_new, slopes, *, pos, page_table=None, pages_per_step=1,
                window=None, sink=None, mask=None, diff=None, out_dtype=F32):
    b, n_grp, n_maps, n_rep, d = q.shape
    dv = v_new.shape[-1]
    rows = n_maps * n_rep
    hk = n_grp * n_maps
    paged = page_table is not None
    if paged:
        n_pages = pages_per_step
        n_steps = page_table.shape[1] // n_pages
        chunk = n_pages * PAGE
        kpos0 = 0
        total = page_table.shape[1] * PAGE
    else:
        n_pages, n_steps = 1, 1
        chunk = k_main.shape[1] // hk
        kpos0 = pos - chunk
        total = chunk

    def idx(f):
        if paged:
            return lambda bb, st, pt: f(bb, st, pt)
        return lambda bb, st: f(bb, st, None)

    in_specs = [pl.BlockSpec((None, n_grp, rows, n_maps * d), idx(lambda bb, st, pt: (bb, 0, 0, 0)))]
    args = [_block_diag_q(q)]
    for main in (k_main, v_main):
        blk = (None,) + main.shape[1:]
        for p in range(n_pages):
            if paged:
                in_specs.append(pl.BlockSpec(blk, lambda bb, st, pt, p=p: (pt[bb, st * n_pages + p], 0, 0)))
            else:
                in_specs.append(pl.BlockSpec(blk, lambda bb, st: (bb, 0, 0)))
            args.append(main)
    col = lambda x: jnp.tile(x.astype(F32).reshape(n_grp, 1, n_rep), (1, n_maps, 1)).reshape(n_grp, rows, 1)
    in_specs += [pl.BlockSpec((None, n_grp, 1, n_maps * d), idx(lambda bb, st, pt: (bb, 0, 0, 0))),
                 pl.BlockSpec((None, n_grp, 1, dv), idx(lambda bb, st, pt: (bb, 0, 0, 0))),
                 pl.BlockSpec((n_grp, rows, 1), idx(lambda bb, st, pt: (0, 0, 0)))]
    args += [k_new, v_new, col(slopes)]
    if sink is not None:
        in_specs.append(pl.BlockSpec((n_grp, rows, 1), idx(lambda bb, st, pt: (0, 0, 0))))
        args.append(col(sink))
    if mask is not None:
        in_specs.append(pl.BlockSpec((None, n_grp, 1, chunk), idx(lambda bb, st, pt: (bb, 0, 0, st))))
        in_specs.append(pl.BlockSpec((None, n_grp, 1, LANE), idx(lambda bb, st, pt: (bb, 0, 0, total // LANE))))
        args += [mask, mask]
    lam_init = 0.0
    if diff is not None:
        lam_vecs, subln, lam_init = diff
        in_specs.append(pl.BlockSpec((4, A_HD), idx(lambda bb, st, pt: (0, 0))))
        in_specs.append(pl.BlockSpec((1, dv), idx(lambda bb, st, pt: (0, 0))))
        args += [lam_vecs.astype(F32), subln.reshape(1, dv).astype(F32)]
    out_spec = pl.BlockSpec((None, n_grp, n_rep, dv), idx(lambda bb, st, pt: (bb, 0, 0, 0)))
    scratch = [pltpu.VMEM((n_grp, rows, 1), F32), pltpu.VMEM((n_grp, rows, 1), F32),
               pltpu.VMEM((n_grp, rows, dv), F32)]
    body = functools.partial(
        _decode_kernel, n_grp=n_grp, n_maps=n_maps, n_rep=n_rep, dv=dv, n_pages=n_pages, n_steps=n_steps,
        paged=paged, chunk=chunk, kpos0=kpos0, pos=pos, window=window, scale=d ** -0.5,
        has_mask=mask is not None, has_sink=sink is not None, diff=diff is not None, lam_init=lam_init)
    out_shape = jax.ShapeDtypeStruct((b, n_grp, n_rep, dv), out_dtype)
    cp = _cparams(("parallel", "arbitrary"), 48)
    if paged:
        return pl.pallas_call(
            body,
            grid_spec=pltpu.PrefetchScalarGridSpec(num_scalar_prefetch=1, grid=(b, n_steps), in_specs=in_specs,
                                                   out_specs=out_spec, scratch_shapes=scratch),
            out_shape=out_shape, compiler_params=cp, name="decode_paged",
        )(page_table, *args)
    return pl.pallas_call(body, grid=(b, n_steps), in_specs=in_specs, out_specs=out_spec, out_shape=out_shape,
                          scratch_shapes=scratch, compiler_params=cp, name="decode_dense")(*args)


def _pool_compress_kernel(x_ref, w_ref, o_ref, *, n_pages, n_grp, d):
    halves = PAGE // NSA_BLK
    page_rows = PAGE * n_grp
    acc = jnp.zeros((halves * n_grp * n_pages, d), F32)
    for j in range(NSA_BLK):
        a = jnp.concatenate([x_ref[pl.ds((h * NSA_BLK + j) * n_grp + g, n_pages, stride=page_rows), :]
                             for h in range(halves) for g in range(n_grp)], axis=0).astype(BF16)
        acc = acc + jnp.dot(a, w_ref[j], preferred_element_type=F32)
    for hg in range(halves * n_grp):
        o_ref[:, hg * d:(hg + 1) * d] = acc[hg * n_pages:(hg + 1) * n_pages]


def pool_compress(cache, w_cmp):
    npool, _, n_grp, d = cache.shape
    n_pages = _pick(npool, (64, 32, 16, 8))
    halves = PAGE // NSA_BLK
    return pl.pallas_call(
        functools.partial(_pool_compress_kernel, n_pages=n_pages, n_grp=n_grp, d=d),
        grid=(npool // n_pages,),
        in_specs=[pl.BlockSpec((n_pages * PAGE * n_grp, d), lambda i: (i, 0)),
                  pl.BlockSpec((NSA_BLK, d, d), lambda i: (0, 0, 0))],
        out_specs=pl.BlockSpec((n_pages, halves * n_grp * d), lambda i: (i, 0)),
        out_shape=jax.ShapeDtypeStruct((npool, halves * n_grp * d), F32),
        compiler_params=_cparams(("parallel",), 40),
        name="pool_compress",
    )(cache.reshape(npool * PAGE * n_grp, d), w_cmp.astype(BF16))


def _nsa_cmp_sample_kernel(pt_ref, q_ref, kpool_ref, vpool_ref, kn_ref, vn_ref, wk0_ref, wv0_ref, slope_ref,
                           e_ref, o_ref, mask_ref, kc_scr, vc_scr, *, n_grp, n_rep, n_pg, nbp, n_sel, pos,
                           scale, ids):
    b = pl.program_id(0)
    d = q_ref.shape[-1]
    gw = n_grp * d
    for p in range(n_pg):
        pid = pt_ref[b, p]
        kc_scr[pl.ds(p, 1), :] = kpool_ref[pl.ds(pid, 1), :]
        vc_scr[pl.ds(p, 1), :] = vpool_ref[pl.ds(pid, 1), :]
    nb = 2 * n_pg + 1
    col = lax.broadcasted_iota(jnp.int32, (1, nbp), 1)
    nid = jnp.where(col < n_pg, 2 * col, jnp.where(col < 2 * n_pg, 2 * (col - n_pg) + 1,
                                                   jnp.where(col == 2 * n_pg, 2 * n_pg, 1 << 20)))
    real = col < nb
    dist = pos - (nid * NSA_BLK + NSA_BLK - 1)
    valid = (dist >= 0) & real
    distf = jnp.where(real, dist, 0).astype(F32)
    cur = pos // NSA_BLK
    forced = (nid == 0) | (nid == cur) | (nid == cur - 1)
    causal = nid * NSA_BLK <= pos
    kn_cmp = jnp.dot(kn_ref[...].astype(BF16), wk0_ref[...], preferred_element_type=F32)
    vn_cmp = jnp.dot(vn_ref[...].astype(BF16), wv0_ref[...], preferred_element_type=F32)
    tail = nbp - 2 * n_pg
    first_row = lax.broadcasted_iota(jnp.int32, (tail, d), 0) == 0
    sels = []
    for g in range(n_grp):
        def blocks(scr, new):
            new_blk = jnp.where(first_row, jnp.broadcast_to(new[g:g + 1], (tail, d)), 0.0)
            return jnp.concatenate([scr[:, g * d:(g + 1) * d], scr[:, gw + g * d:gw + (g + 1) * d],
                                    new_blk], axis=0).astype(BF16)
        kc = blocks(kc_scr, kn_cmp)
        vc = blocks(vc_scr, vn_cmp)
        s = lax.dot_general(q_ref[g], kc, (((1,), (1,)), ((), ())), preferred_element_type=F32)
        s = s * scale - slope_ref[g] * distf
        s = jnp.where(valid, s, NEG_INF)
        m = jnp.max(s, axis=-1, keepdims=True)
        e = jnp.where(valid, jnp.exp(s - m), 0.0)
        p = e / jnp.maximum(jnp.sum(e, axis=-1, keepdims=True), TINY)
        o_ref[g] = jnp.dot(p.astype(BF16), vc, preferred_element_type=F32)
        score = jnp.sum(p, axis=0, keepdims=True)
        score = jnp.where(forced, NSA_FORCED, score)
        score = jnp.where(causal, score, -1.0)
        score = jnp.where(real, score, -2.0)
        sels.append(_rank_select(score, nid, n_sel, ids))
    sel = jnp.concatenate(sels + [jnp.zeros((8 - n_grp, nbp), F32)], axis=0)
    km = jnp.dot(sel.astype(BF16), e_ref[...], preferred_element_type=F32)
    for g in range(n_grp):
        mask_ref[g] = km[g:g + 1]


def nsa_cmp_sample(q, kpool, vpool, kc_new, vc_new, wk0, wv0, slopes, page_table, pos):
    b, n_grp, n_rep, d = q.shape
    n_pg = page_table.shape[1]
    nb = 2 * n_pg + 1
    nbp = _round_up(nb, LANE)
    total = n_pg * PAGE
    ids = [2 * s for s in range(n_pg)] + [2 * s + 1 for s in range(n_pg)] + [2 * n_pg]
    slot_of_block = {bid: s for s, bid in enumerate(ids)}
    key_slot = jnp.asarray([slot_of_block[kp // NSA_BLK] for kp in range(total)]
                           + [slot_of_block[2 * n_pg]] + [-1] * (LANE - 1), jnp.int32)
    expand = (jnp.arange(nbp, dtype=jnp.int32)[:, None] == key_slot[None, :]).astype(BF16)
    npool = kpool.shape[0]
    full = lambda shape: pl.BlockSpec(shape, lambda bb, pt: tuple(0 for _ in shape))
    return pl.pallas_call(
        functools.partial(_nsa_cmp_sample_kernel, n_grp=n_grp, n_rep=n_rep, n_pg=n_pg, nbp=nbp,
                          n_sel=min(NSA_TOPK, nb), pos=pos, scale=d ** -0.5, ids=ids),
        grid_spec=pltpu.PrefetchScalarGridSpec(
            num_scalar_prefetch=1, grid=(b,),
            in_specs=[pl.BlockSpec((None, n_grp, n_rep, d), lambda bb, pt: (bb, 0, 0, 0)),
                      full((npool, 2 * n_grp * d)), full((npool, 2 * n_grp * d)),
                      pl.BlockSpec((None, 8, d), lambda bb, pt: (bb, 0, 0)),
                      pl.BlockSpec((None, 8, d), lambda bb, pt: (bb, 0, 0)),
                      full((d, d)), full((d, d)), full((n_grp, n_rep, 1)), full((nbp, total + LANE))],
            out_specs=[pl.BlockSpec((None, n_grp, n_rep, d), lambda bb, pt: (bb, 0, 0, 0)),
                       pl.BlockSpec((None, n_grp, 1, total + LANE), lambda bb, pt: (bb, 0, 0, 0))],
            scratch_shapes=[pltpu.VMEM((n_pg, 2 * n_grp * d), F32), pltpu.VMEM((n_pg, 2 * n_grp * d), F32)]),
        out_shape=[jax.ShapeDtypeStruct((b, n_grp, n_rep, d), F32),
                   jax.ShapeDtypeStruct((b, n_grp, 1, total + LANE), F32)],
        compiler_params=_cparams(("arbitrary",), 56),
        name="nsa_cmp_sample",
    )(page_table, q, kpool, vpool, kc_new, vc_new, wk0, wv0, slopes.astype(F32).reshape(n_grp, n_rep, 1), expand)


def diff_layer(xp, xs, hp, hs, bsz, tlen, past, cache_k, cache_v, page_table, w_in, lam_vecs, subln, w_o,
               layer_idx):
    lam_init = 0.8 - 0.6 * math.exp(-0.3 * layer_idx)
    slopes = alibi_slopes(A_HEADS, A_KV)
    n_rep = A_HEADS // A_KV
    a_q, a_kw = A_HEADS * 2 * A_HD, A_KV * 2 * A_HD
    w_in_b, w_o_b = w_in.astype(BF16), w_o.astype(BF16)
    diff = (lam_vecs, subln, lam_init)

    proj = matmul(hp, w_in_b).reshape(bsz, tlen, a_q + 2 * a_kw)
    kp = proj[..., a_q:a_q + a_kw].reshape(bsz, tlen, A_KV, 2, A_HD)
    vp = proj[..., a_q + a_kw:].reshape(bsz, tlen, A_KV, 2 * A_HD)
    o = causal_prompt(proj, slopes, n_grp=A_KV, n_rep=n_rep, n_maps=2, d=A_HD, dv=2 * A_HD, q_col=0, k_col=a_q,
                      v_col=a_q + a_kw, diff=diff, out_dtype=BF16)
    xp = matmul(o.reshape(bsz * tlen, a_q), w_o_b, res=xp)

    nb = hs.shape[0]
    npool = cache_k.shape[0]
    proj_s = matmul(hs, w_in_b)
    qs = proj_s[:, :a_q].reshape(nb, A_KV, n_rep, 2, A_HD).transpose(0, 1, 3, 2, 4).astype(BF16)
    ks = proj_s[:, a_q:a_q + a_kw].reshape(nb, 1, A_KV, 2, A_HD)
    vs = proj_s[:, a_q + a_kw:].reshape(nb, 1, A_KV, 2 * A_HD)
    v_rows = cache_v.reshape(npool, PAGE, A_KV, 2, A_HD).transpose(0, 1, 3, 2, 4).reshape(npool, PAGE * 2 * A_KV, A_HD)
    os_ = decode_attn(qs, cache_k.reshape(npool, PAGE * A_KV * 2, A_HD), v_rows,
                      ks.reshape(nb, A_KV, 1, 2 * A_HD), vs.reshape(nb, A_KV, 1, 2 * A_HD),
                      slopes, pos=past, page_table=page_table, pages_per_step=8, diff=diff, out_dtype=BF16)
    xs = matmul(os_.reshape(nb, a_q), w_o_b, res=xs)
    return xp, xs, (kp, ks, vp, vs)


def _cmp_weight(w_cmp):
    blk, d, _ = w_cmp.shape
    eye = jnp.eye(B_KV, dtype=w_cmp.dtype)
    return jnp.einsum('jde,gh->jgdhe', w_cmp, eye).reshape(blk * B_KV * d, B_KV * d).astype(BF16)


def nsa_layer(xp, xs, hp, hs, bsz, tlen, past, cache_kc, cache_vc, cache_ks, cache_vs, buf_kw, buf_vw,
              page_table, w_in, b_gate, w_cmp_k, w_cmp_v, w_o):
    slopes = alibi_slopes(B_HEADS, B_KV)
    n_rep = B_HEADS // B_KV
    b_q, b_kw = B_HEADS * B_HD, B_KV * B_HD
    n_in = w_in.shape[1]
    n_pad = _round_up(n_in, 1024)
    gate_col = b_q + 6 * b_kw
    w_in_b = jnp.pad(w_in, ((0, 0), (0, n_pad - n_in))).astype(BF16)
    w_o_b = w_o.astype(BF16)
    wck, wcv = _cmp_weight(w_cmp_k), _cmp_weight(w_cmp_v)
    kv_col = lambda i: b_q + i * b_kw

    proj = matmul(hp, w_in_b)
    proj3 = proj.reshape(bsz, tlen, n_pad)
    kv = [proj3[..., kv_col(i):kv_col(i + 1)].reshape(bsz, tlen, B_KV, B_HD) for i in range(6)]
    nblk = tlen // NSA_BLK
    cmp_in = lambda a: a.reshape(bsz * nblk, NSA_BLK * b_kw).astype(BF16)
    kc_cmp = matmul(cmp_in(kv[0]), wck).reshape(bsz, nblk, B_KV, B_HD).transpose(0, 2, 1, 3).astype(BF16)
    vc_cmp = matmul(cmp_in(kv[1]), wcv).reshape(bsz, nblk, B_KV, B_HD).transpose(0, 2, 1, 3).astype(BF16)
    o_cmp, sel = nsa_cmp_prompt(proj3, kc_cmp, vc_cmp, slopes, n_grp=B_KV, n_rep=n_rep, d=B_HD)
    expand = (jnp.arange(_round_up(nblk, LANE), dtype=jnp.int32)[:, None]
              == (jnp.arange(tlen, dtype=jnp.int32) // NSA_BLK)[None, :]).astype(BF16)
    o_sel = causal_prompt(proj3, slopes, n_grp=B_KV, n_rep=n_rep, n_maps=1, d=B_HD, dv=B_HD, q_col=0,
                          k_col=kv_col(2), v_col=kv_col(3), sel=(sel, expand))
    o_win = banded_prompt(proj3, proj3, proj3, slopes, window=NSA_WIN, tq=NSA_WIN, n_grp=B_KV, n_rep=n_rep,
                          d=B_HD, dv=B_HD, q_col=0, k_col=kv_col(4), v_col=kv_col(5), rep_chunk=4)
    m = bsz * tlen
    o = nsa_combine(o_cmp.reshape(m, b_q), o_sel.reshape(m, b_q), o_win.reshape(m, b_q), proj, b_gate, gate_col)
    xp = matmul(o, w_o_b, res=xp)

    nb = hs.shape[0]
    proj_s = matmul(hs, w_in_b)
    qs = proj_s[:, :b_q].reshape(nb, B_KV, n_rep, B_HD).astype(BF16)
    kvs = [proj_s[:, kv_col(i):kv_col(i + 1)].reshape(nb, 1, B_KV, B_HD) for i in range(6)]
    kpool = pool_compress(cache_kc, w_cmp_k)
    vpool = pool_compress(cache_vc, w_cmp_v)
    new8 = lambda a: jnp.pad(a.reshape(nb, B_KV, B_HD), ((0, 0), (0, 8 - B_KV), (0, 0)))
    o_cmp_s, keymask = nsa_cmp_sample(qs, kpool, vpool, new8(kvs[0]), new8(kvs[1]),
                                      w_cmp_k[0].astype(BF16), w_cmp_v[0].astype(BF16), slopes, page_table, past)
    qs5 = qs[:, :, None]
    new_row = lambda a: a.reshape(nb, B_KV, 1, B_HD)
    head_rows = lambda c: c.reshape(c.shape[0], c.shape[1] * B_KV, B_HD)
    o_sel_s = decode_attn(qs5, head_rows(cache_ks), head_rows(cache_vs), new_row(kvs[2]), new_row(kvs[3]), slopes,
                          pos=past, page_table=page_table, pages_per_step=8, mask=keymask)
    o_win_s = decode_attn(qs5, head_rows(buf_kw), head_rows(buf_vw), new_row(kvs[4]), new_row(kvs[5]), slopes,
                          pos=past, window=NSA_WIN)
    os_ = nsa_combine(o_cmp_s.reshape(nb, b_q), o_sel_s.reshape(nb, b_q), o_win_s.reshape(nb, b_q), proj_s,
                      b_gate, gate_col)
    xs = matmul(os_, w_o_b, res=xs)

    wb = buf_kw.shape[1]
    wp = min(NSA_WIN, tlen)
    kw_all = jnp.concatenate([buf_kw, kvs[4]], axis=1)
    vw_all = jnp.concatenate([buf_vw, kvs[5]], axis=1)
    state = (kv[0], kvs[0], kv[1], kvs[1], kv[2], kvs[2], kv[3], kvs[3],
             kv[4][:, tlen - wp:], kw_all[:, -wb:], kv[5][:, tlen - wp:], vw_all[:, -wb:])
    return xp, xs, state


def swa_layer(xp, xs, hp, hs, bsz, tlen, past, buf_k, buf_v, w_in, b_in, sinks, w_o, b_o):
    slopes = alibi_slopes(C_HEADS, C_KV)
    sink = sinks.astype(F32).reshape(C_KV, C_HEADS // C_KV)
    n_rep = C_HEADS // C_KV
    c_q, c_kw = C_HEADS * C_HD, C_KV * C_HD
    w_in_b, w_o_b = w_in.astype(BF16), w_o.astype(BF16)
    heads = lambda a: a.transpose(0, 2, 1, 3).astype(BF16)

    proj = matmul(hp, w_in_b, bias=b_in).reshape(bsz, tlen, c_q + 2 * c_kw)
    q = proj[..., :c_q].reshape(bsz, tlen, C_KV, n_rep, C_HD).transpose(0, 2, 3, 1, 4).astype(BF16)
    kp = proj[..., c_q:c_q + c_kw].reshape(bsz, tlen, C_KV, C_HD)
    vp = proj[..., c_q + c_kw:].reshape(bsz, tlen, C_KV, C_HD)
    o = banded_prompt(q, heads(kp), heads(vp), slopes, window=C_WIN, tq=C_WIN, n_grp=C_KV, n_rep=n_rep, d=C_HD,
                      dv=C_HD, sink=sink, out_dtype=BF16)
    o = o.transpose(0, 3, 1, 2, 4).reshape(bsz * tlen, c_q)
    xp = matmul(o, w_o_b, bias=b_o, res=xp)

    nb = hs.shape[0]
    proj_s = matmul(hs, w_in_b, bias=b_in)
    qs = proj_s[:, :c_q].reshape(nb, C_KV, 1, n_rep, C_HD).astype(BF16)
    ks = proj_s[:, c_q:c_q + c_kw].reshape(nb, 1, C_KV, C_HD)
    vs = proj_s[:, c_q + c_kw:].reshape(nb, 1, C_KV, C_HD)
    head_rows = lambda c: c.reshape(c.shape[0], c.shape[1] * C_KV, C_HD)
    os_ = decode_attn(qs, head_rows(buf_k), head_rows(buf_v), ks.reshape(nb, C_KV, 1, C_HD),
                      vs.reshape(nb, C_KV, 1, C_HD), slopes, pos=past, window=C_WIN, sink=sink, out_dtype=BF16)
    xs = matmul(os_.reshape(nb, c_q), w_o_b, bias=b_o, res=xs)

    wb = buf_k.shape[1]
    wp = min(C_WIN, tlen)
    k_all = jnp.concatenate([buf_k, ks], axis=1)
    v_all = jnp.concatenate([buf_v, vs], axis=1)
    return xp, xs, (kp[:, tlen - wp:], k_all[:, -wb:], vp[:, tlen - wp:], v_all[:, -wb:])


def kernel(x_prompt, x_sample, cache_l0_k, cache_l0_v, cache_l1_kc, cache_l1_vc, cache_l1_ks, cache_l1_vs, state_l1_kw, state_l1_vw, state_l2_k, state_l2_v, cache_l3_k, cache_l3_v, page_table, norm_ffa, norm_mix, norm_ffb, norm_final, ffa_w13, ffa_w2, ffb_w13, ffb_w2, l0_w_in, l0_lam, l0_subln, l0_w_o, l1_w_in, l1_b_gate, l1_w_cmp_k, l1_w_cmp_v, l1_w_o, l2_w_in, l2_b_in, l2_sinks, l2_w_o, l2_b_o, l3_w_in, l3_lam, l3_subln, l3_w_o):
    bsz, tlen, dm = x_prompt.shape
    nb, dec_seq, _ = x_sample.shape
    assert dec_seq == 1 and dm == D_MODEL
    past = page_table.shape[1] * PAGE
    xp = x_prompt.reshape(bsz * tlen, dm)
    xs = x_sample.reshape(nb, dm)
    ffa13, ffb13 = ffa_w13.astype(BF16), ffb_w13.astype(BF16)
    ffa2, ffb2 = ffa_w2.astype(BF16), ffb_w2.astype(BF16)
    a_params = {0: (cache_l0_k, cache_l0_v, l0_w_in, l0_lam, l0_subln, l0_w_o),
                3: (cache_l3_k, cache_l3_v, l3_w_in, l3_lam, l3_subln, l3_w_o)}
    state = {}
    for i in range(4):
        xp, xs = ffn_block(xp, xs, norm_ffa[i], ffa13, ffa2, i)
        hp = rmsnorm(xp, norm_mix[i], BF16)
        hs = rmsnorm(xs, norm_mix[i], BF16)
        kind = i % 3
        if kind == 0:
            ck, cv, w_in, lam, subln, w_o = a_params[i]
            xp, xs, st = diff_layer(xp, xs, hp, hs, bsz, tlen, past, ck, cv, page_table, w_in, lam, subln, w_o, i)
        elif kind == 1:
            xp, xs, st = nsa_layer(xp, xs, hp, hs, bsz, tlen, past, cache_l1_kc, cache_l1_vc, cache_l1_ks,
                                   cache_l1_vs, state_l1_kw, state_l1_vw, page_table, l1_w_in, l1_b_gate,
                                   l1_w_cmp_k, l1_w_cmp_v, l1_w_o)
        else:
            xp, xs, st = swa_layer(xp, xs, hp, hs, bsz, tlen, past, state_l2_k, state_l2_v, l2_w_in, l2_b_in,
                                   l2_sinks, l2_w_o, l2_b_o)
        state[i] = st
        xp, xs = ffn_block(xp, xs, norm_ffb[i], ffb13, ffb2, i)
    y_prompt = rmsnorm(xp, norm_final, F32).reshape(bsz, tlen, dm)
    y_sample = rmsnorm(xs, norm_final, F32).reshape(nb, 1, dm)
    return (y_prompt, y_sample) + tuple(state[0]) + tuple(state[1]) + tuple(state[2]) + tuple(state[3])
```

```python
import functools
import math

import jax
import jax.numpy as jnp
from jax import lax
from jax.experimental import pallas as pl
from jax.experimental.pallas import tpu as pltpu

NORM_EPS = 1e-6
NEG_INF = -1e30
TINY = 1e-30
LOG2E = 1.4426950408889634

D_MODEL = 4096
PAGE = 128
A_HD, A_HEADS, A_KV = 128, 16, 4
B_HD, B_HEADS, B_KV = 128, 32, 2
NSA_BLK, NSA_TOPK, NSA_WIN, NSA_FORCED = 64, 16, 512, 1e4
C_HD, C_HEADS, C_KV, C_WIN = 64, 64, 8, 128

LANE = 128
QT = 256
KT = 512
FFN_TN = 256
FULL_K_TILE_BYTES = 12 * 1024 * 1024
MIB = 1024 * 1024
BF16 = jnp.bfloat16
F32 = jnp.float32


def _cparams(sem, vmem_mib):
    return pltpu.CompilerParams(dimension_semantics=sem, vmem_limit_bytes=vmem_mib * MIB)


def _round_up(x, m):
    return -(-x // m) * m


def _pick(n, prefs):
    for p in prefs:
        if n % p == 0:
            return p
    return n


def _rep(x, width):
    return x if width == LANE else jnp.concatenate([x] * (width // LANE), axis=1)


def _rms_kernel(x_ref, g_ref, o_ref):
    x = x_ref[...]
    ms = jnp.mean(x * x, axis=-1, keepdims=True)
    o_ref[...] = (x * lax.rsqrt(ms + NORM_EPS) * g_ref[...]).astype(o_ref.dtype)


def rmsnorm(x, g, out_dtype):
    m, d = x.shape
    tm = _pick(m, (256, 128, 64, 32, 16, 8))
    return pl.pallas_call(
        _rms_kernel,
        grid=(m // tm,),
        in_specs=[pl.BlockSpec((tm, d), lambda i: (i, 0)), pl.BlockSpec((1, d), lambda i: (0, 0))],
        out_specs=pl.BlockSpec((tm, d), lambda i: (i, 0)),
        out_shape=jax.ShapeDtypeStruct((m, d), out_dtype),
        compiler_params=_cparams(("parallel",), 40),
        name="rmsnorm",
    )(x, g.reshape(1, d).astype(F32))


def _mm_kernel(*refs, nk, has_bias, has_res, res_scale):
    it = iter(refs)
    a_ref, w_ref = next(it), next(it)
    b_ref = next(it) if has_bias else None
    r_ref = next(it) if has_res else None
    o_ref = next(it)
    acc_ref = next(it) if nk > 1 else None

    def epilogue(acc):
        if has_bias:
            acc = acc + b_ref[...]
        if has_res:
            acc = r_ref[...] + res_scale * acc
        o_ref[...] = acc.astype(o_ref.dtype)

    d = jnp.dot(a_ref[...], w_ref[...].astype(BF16), preferred_element_type=F32)
    if nk == 1:
        epilogue(d)
    else:
        k = pl.program_id(2)

        @pl.when(k == 0)
        def _():
            acc_ref[...] = d

        @pl.when(k > 0)
        def _():
            acc_ref[...] += d

        @pl.when(k == nk - 1)
        def _():
            epilogue(acc_ref[...])


def matmul(a, w, *, layer=None, bias=None, res=None, res_scale=1.0, out_dtype=F32):
    m, k = a.shape
    n = w.shape[-1]
    tn = _pick(n, (512, 256, 128))
    if k <= 4096:
        tm, tk = _pick(m, (1024, 512, 256, 128, 64, 32, 16, 8)), k
    elif k * tn * 2 <= FULL_K_TILE_BYTES:
        tm, tk = _pick(m, (512, 256, 128, 64, 32, 16, 8)), k
    else:
        tm, tk = _pick(m, (1024, 512, 256, 128, 64, 32, 16, 8)), _pick(k, (2048, 1024, 512, 256, 128))
    nk = k // tk
    if w.ndim == 3:
        w_spec = pl.BlockSpec((None, tk, tn), lambda i, j, kk: (layer, kk, j))
    else:
        w_spec = pl.BlockSpec((tk, tn), lambda i, j, kk: (kk, j))
    in_specs = [pl.BlockSpec((tm, tk), lambda i, j, kk: (i, kk)), w_spec]
    args = [a, w]
    if bias is not None:
        in_specs.append(pl.BlockSpec((1, tn), lambda i, j, kk: (0, j)))
        args.append(bias.reshape(1, n).astype(F32))
    if res is not None:
        in_specs.append(pl.BlockSpec((tm, tn), lambda i, j, kk: (i, j)))
        args.append(res)
    return pl.pallas_call(
        functools.partial(_mm_kernel, nk=nk, has_bias=bias is not None, has_res=res is not None,
                          res_scale=res_scale),
        grid=(m // tm, n // tn, nk),
        in_specs=in_specs,
        out_specs=pl.BlockSpec((tm, tn), lambda i, j, kk: (i, j)),
        out_shape=jax.ShapeDtypeStruct((m, n), out_dtype),
        scratch_shapes=[pltpu.VMEM((tm, tn), F32)] if nk > 1 else [],
        compiler_params=_cparams(("parallel", "parallel", "arbitrary"), 56),
        name="matmul",
    )(*args)


def _ffn1_kernel(a_ref, wg_ref, wu_ref, o_ref):
    a = a_ref[...]
    g = jnp.dot(a, wg_ref[...].astype(BF16), preferred_element_type=F32)
    u = jnp.dot(a, wu_ref[...].astype(BF16), preferred_element_type=F32)
    o_ref[...] = (g / (1.0 + jnp.exp(-g)) * u).astype(o_ref.dtype)


def ffn1(a, w13, layer):
    m, k = a.shape
    f = w13.shape[-1] // 2
    tn = _pick(f, (FFN_TN, LANE))
    n_tiles = f // tn
    tm = _pick(m, (2048, 1024, 512, 256, 128, 64, 32, 16, 8))
    return pl.pallas_call(
        _ffn1_kernel,
        grid=(m // tm, n_tiles),
        in_specs=[pl.BlockSpec((tm, k), lambda i, j: (i, 0), pipeline_mode=pl.Buffered(1)),
                  pl.BlockSpec((None, k, tn), lambda i, j: (layer, 0, j)),
                  pl.BlockSpec((None, k, tn), lambda i, j: (layer, 0, j + n_tiles))],
        out_specs=pl.BlockSpec((tm, tn), lambda i, j: (i, j)),
        out_shape=jax.ShapeDtypeStruct((m, f), BF16),
        compiler_params=_cparams(("parallel", "parallel"), 56),
        name="ffn1",
    )(a, w13, w13)


def ffn_block(xp, xs, g, w13, w2, layer):
    outs = []
    for x in (xp, xs):
        h = rmsnorm(x, g, BF16)
        mid = ffn1(h, w13, layer)
        outs.append(matmul(mid, w2, layer=layer, res=x, res_scale=0.5))
    return outs


def alibi_slopes(n_heads, n_groups):
    i = jnp.arange(1, n_heads + 1, dtype=F32)
    return jnp.exp2(-8.0 * i / n_heads).reshape(n_groups, n_heads // n_groups)


def _diff_finish(o0, o1, lam_ref, subln_ref, lam_init):
    lv = lam_ref[...]
    lam = (jnp.exp(jnp.sum(lv[0:1] * lv[1:2], axis=-1, keepdims=True))
           - jnp.exp(jnp.sum(lv[2:3] * lv[3:4], axis=-1, keepdims=True)) + lam_init)
    o = o0 - lam * o1
    ms = jnp.mean(o * o, axis=-1, keepdims=True)
    return o * lax.rsqrt(ms + NORM_EPS) * subln_ref[...] * (1.0 - lam_init)


def _causal_kernel(*refs, n_maps, n_rep, d, dv, nkt, scale, has_sel, diff, lam_init):
    it = iter(refs)
    slope_ref = next(it)
    q_ref, k_ref, v_ref = next(it), next(it), next(it)
    sel_ref, e_ref = (next(it), next(it)) if has_sel else (None, None)
    lam_ref, subln_ref = (next(it), next(it)) if diff else (None, None)
    o_ref = next(it)
    q_scr, m_ref, l_ref, acc_ref, e_scr, a_scr = (next(it) for _ in range(6))

    tq, tk = QT, KT
    ratio = tk // tq
    g = pl.program_id(1)
    i = pl.program_id(2)
    kt = pl.program_id(3)
    diag = i // ratio

    @pl.when(kt == 0)
    def _():
        m_ref[...] = jnp.full(m_ref.shape, NEG_INF, F32)
        l_ref[...] = jnp.zeros(l_ref.shape, F32)
        acc_ref[...] = jnp.zeros(acc_ref.shape, F32)
        for c in range(n_maps):
            for r in range(n_rep):
                col = (r * n_maps + c) * d
                q_scr[c, r * tq:(r + 1) * tq, :] = q_ref[:, col:col + d].astype(BF16)

    def step(causal):
        kposf = (kt * tk - i * tq + lax.broadcasted_iota(jnp.int32, (1, tk), 1)).astype(F32)
        mask = None
        if causal:
            mask = (kt * tk + lax.broadcasted_iota(jnp.int32, (tq, tk), 1)
                    <= i * tq + lax.broadcasted_iota(jnp.int32, (tq, tk), 0))
        if has_sel:
            picked = jnp.dot(sel_ref[...].astype(BF16), e_ref[...], preferred_element_type=F32) > 0.5
            mask = picked if mask is None else mask & picked
        kb = k_ref[...].astype(BF16)
        vb = v_ref[...].astype(BF16)
        for c in range(n_maps):
            s_all = lax.dot_general(q_scr[c], kb[:, c * d:(c + 1) * d], (((1,), (1,)), ((), ())),
                                    preferred_element_type=F32)
            for r in range(n_rep):
                sl = slice(r * tq, (r + 1) * tq)
                s = s_all[sl] * (scale * LOG2E) + (slope_ref[g * n_rep + r] * LOG2E) * kposf
                if mask is not None:
                    s = jnp.where(mask, s, NEG_INF)
                m_old = m_ref[c, sl]
                m_new = jnp.maximum(m_old, jnp.max(s, axis=-1, keepdims=True))
                e = jnp.exp2(s - _rep(m_new, tk))
                if has_sel:
                    e = jnp.where(mask, e, 0.0)
                alpha = jnp.exp2(m_old - m_new)
                l_ref[c, sl] = alpha * l_ref[c, sl] + jnp.sum(e, axis=-1, keepdims=True)
                m_ref[c, sl] = m_new
                a_scr[sl] = alpha
                e_scr[sl] = e.astype(BF16)
            acc_ref[c] = _rep(a_scr[...], dv) * acc_ref[c] + jnp.dot(e_scr[...], vb, preferred_element_type=F32)

    @pl.when(kt < diag)
    def _():
        step(False)

    @pl.when(kt == diag)
    def _():
        step(True)

    @pl.when(kt == nkt - 1)
    def _():
        o = acc_ref[0] / jnp.maximum(_rep(l_ref[0], dv), TINY)
        if diff:
            o = _diff_finish(o, acc_ref[1] / jnp.maximum(_rep(l_ref[1], dv), TINY), lam_ref, subln_ref, lam_init)
        for r in range(n_rep):
            o_ref[:, r * dv:(r + 1) * dv] = o[r * tq:(r + 1) * tq].astype(o_ref.dtype)


def causal_prompt(proj, slopes, *, n_grp, n_rep, n_maps, d, dv, q_col, k_col, v_col, sel=None, diff=None,
                  out_dtype=F32):
    b, tlen, _ = proj.shape
    tq, tk = QT, KT
    ratio = tk // tq
    nq, nkt = tlen // tq, tlen // tk
    rows = n_rep * tq
    qw, kw = n_rep * n_maps * d, n_maps * d
    kidx = lambda i, kt: jnp.minimum(kt, i // ratio)
    in_specs = [
        pl.BlockSpec(memory_space=pltpu.SMEM),
        pl.BlockSpec((None, tq, qw), lambda bb, gg, i, kt: (bb, i, q_col // qw + gg)),
        pl.BlockSpec((None, tk, kw), lambda bb, gg, i, kt: (bb, kidx(i, kt), k_col // kw + gg)),
        pl.BlockSpec((None, tk, dv), lambda bb, gg, i, kt: (bb, kidx(i, kt), v_col // dv + gg)),
    ]
    args = [slopes.reshape(-1).astype(F32), proj, proj, proj]
    if sel is not None:
        sel_mask, expand = sel
        nbp = sel_mask.shape[-1]
        in_specs.append(pl.BlockSpec((None, None, tq, nbp), lambda bb, gg, i, kt: (bb, gg, i, 0)))
        in_specs.append(pl.BlockSpec((nbp, tk), lambda bb, gg, i, kt: (0, kidx(i, kt))))
        args += [sel_mask, expand]
    lam_init = 0.0
    if diff is not None:
        lam_vecs, subln, lam_init = diff
        in_specs.append(pl.BlockSpec((4, A_HD), lambda bb, gg, i, kt: (0, 0)))
        in_specs.append(pl.BlockSpec((1, dv), lambda bb, gg, i, kt: (0, 0)))
        args += [lam_vecs.astype(F32), subln.reshape(1, dv).astype(F32)]
    return pl.pallas_call(
        functools.partial(_causal_kernel, n_maps=n_maps, n_rep=n_rep, d=d, dv=dv, nkt=nkt, scale=d ** -0.5,
                          has_sel=sel is not None, diff=diff is not None, lam_init=lam_init),
        grid=(b, n_grp, nq, nkt),
        in_specs=in_specs,
        out_specs=pl.BlockSpec((None, tq, n_rep * dv), lambda bb, gg, i, kt: (bb, i, gg)),
        out_shape=jax.ShapeDtypeStruct((b, tlen, n_grp * n_rep * dv), out_dtype),
        scratch_shapes=[pltpu.VMEM((n_maps, rows, d), BF16), pltpu.VMEM((n_maps, rows, LANE), F32),
                        pltpu.VMEM((n_maps, rows, LANE), F32), pltpu.VMEM((n_maps, rows, dv), F32),
                        pltpu.VMEM((rows, tk), BF16), pltpu.VMEM((rows, LANE), F32)],
        compiler_params=_cparams(("parallel", "parallel", "parallel", "arbitrary"), 48),
        name="causal_prompt",
    )(*args)


def _banded_kernel(*refs, n_rep, n_chunks, tq, d, dv, window, scale, has_sink, head_major):
    it = iter(refs)
    slope_ref = next(it)
    sink_ref = next(it) if has_sink else None
    q_ref, kp_ref, kc_ref, vp_ref, vc_ref, o_ref = (next(it) for _ in range(6))
    q_scr, e_scr = next(it), next(it)

    tk = 2 * tq
    if head_major:
        g, i = pl.program_id(1), pl.program_id(2)
        head0 = g * n_rep
    else:
        i = pl.program_id(2)
        head0 = pl.program_id(1) * n_rep
    for r in range(n_rep):
        if head_major:
            q_scr[r * tq:(r + 1) * tq, :] = q_ref[r]
        else:
            q_scr[r * tq:(r + 1) * tq, :] = q_ref[:, r * d:(r + 1) * d].astype(BF16)
    kb = jnp.concatenate([kp_ref[...], kc_ref[...]], axis=0).astype(BF16)
    vb = jnp.concatenate([vp_ref[...], vc_ref[...]], axis=0).astype(BF16)
    s_all = lax.dot_general(q_scr[...], kb, (((1,), (1,)), ((), ())), preferred_element_type=F32)
    row = lax.broadcasted_iota(jnp.int32, (tq, tk), 0)
    col = lax.broadcasted_iota(jnp.int32, (tq, tk), 1)
    dist = row + tq - col
    mask = (dist >= 0) & (dist <= window) & ((i - 1) * tq + col >= 0)
    kposf = (lax.broadcasted_iota(jnp.int32, (1, tk), 1) - tq).astype(F32)
    rowf = lax.broadcasted_iota(jnp.int32, (tq, 1), 0).astype(F32)
    for r in range(n_rep):
        sl = slice(r * tq, (r + 1) * tq)
        slope = slope_ref[head0 + r] * LOG2E
        s = s_all[sl] * (scale * LOG2E) + slope * kposf
        s = jnp.where(mask, s, NEG_INF)
        m = jnp.max(s, axis=-1, keepdims=True)
        if has_sink:
            sink = sink_ref[head0 + r] * LOG2E + slope * rowf
            m = jnp.maximum(m, sink)
        e = jnp.exp2(s - m)
        den = jnp.sum(e, axis=-1, keepdims=True)
        if has_sink:
            den = den + jnp.exp2(sink - m)
        e_scr[sl] = (e / jnp.maximum(den, TINY)).astype(BF16)
    o = jnp.dot(e_scr[...], vb, preferred_element_type=F32)
    for r in range(n_rep):
        if head_major:
            o_ref[r] = o[r * tq:(r + 1) * tq].astype(o_ref.dtype)
        else:
            o_ref[:, r * dv:(r + 1) * dv] = o[r * tq:(r + 1) * tq].astype(o_ref.dtype)


def banded_prompt(q, k, v, slopes, *, window, tq, n_grp, n_rep, d, dv, q_col=0, k_col=0, v_col=0, rep_chunk=None,
                  sink=None, out_dtype=F32):
    head_major = rep_chunk is None
    assert window <= tq
    tlen = q.shape[3] if head_major else q.shape[1]
    b = q.shape[0]
    nq = tlen // tq
    prev = lambda i: jnp.maximum(i - 1, 0)
    smem = pl.BlockSpec(memory_space=pltpu.SMEM)
    in_specs, args = [smem], [slopes.reshape(-1).astype(F32)]
    if sink is not None:
        in_specs.append(smem)
        args.append(sink.reshape(-1).astype(F32))
    if head_major:
        rc, n_chunks = n_rep, 1
        grid = (b, n_grp, nq)
        in_specs += [pl.BlockSpec((None, None, rc, tq, d), lambda bb, gg, i: (bb, gg, 0, i, 0)),
                     pl.BlockSpec((None, None, tq, d), lambda bb, gg, i: (bb, gg, prev(i), 0)),
                     pl.BlockSpec((None, None, tq, d), lambda bb, gg, i: (bb, gg, i, 0)),
                     pl.BlockSpec((None, None, tq, dv), lambda bb, gg, i: (bb, gg, prev(i), 0)),
                     pl.BlockSpec((None, None, tq, dv), lambda bb, gg, i: (bb, gg, i, 0))]
        out_spec = pl.BlockSpec((None, None, rc, tq, dv), lambda bb, gg, i: (bb, gg, 0, i, 0))
        out_shape = jax.ShapeDtypeStruct((b, n_grp, n_rep, tlen, dv), out_dtype)
    else:
        rc = rep_chunk
        n_chunks = n_rep // rc
        grid = (b, n_grp * n_chunks, nq)
        kcol = lambda gc: k_col // d + gc // n_chunks
        vcol = lambda gc: v_col // dv + gc // n_chunks
        in_specs += [pl.BlockSpec((None, tq, rc * d), lambda bb, gc, i: (bb, i, q_col // (rc * d) + gc)),
                     pl.BlockSpec((None, tq, d), lambda bb, gc, i: (bb, prev(i), kcol(gc))),
                     pl.BlockSpec((None, tq, d), lambda bb, gc, i: (bb, i, kcol(gc))),
                     pl.BlockSpec((None, tq, dv), lambda bb, gc, i: (bb, prev(i), vcol(gc))),
                     pl.BlockSpec((None, tq, dv), lambda bb, gc, i: (bb, i, vcol(gc)))]
        out_spec = pl.BlockSpec((None, tq, rc * dv), lambda bb, gc, i: (bb, i, gc))
        out_shape = jax.ShapeDtypeStruct((b, tlen, n_grp * n_rep * dv), out_dtype)
    args += [q, k, k, v, v]
    return pl.pallas_call(
        functools.partial(_banded_kernel, n_rep=rc, n_chunks=n_chunks, tq=tq, d=d, dv=dv, window=window,
                          scale=d ** -0.5, has_sink=sink is not None, head_major=head_major),
        grid=grid,
        in_specs=in_specs,
        out_specs=out_spec,
        out_shape=out_shape,
        scratch_shapes=[pltpu.VMEM((rc * tq, d), BF16), pltpu.VMEM((rc * tq, 2 * tq), BF16)],
        compiler_params=_cparams(("parallel", "parallel", "parallel"), 56),
        name="banded_prompt",
    )(*args)


def _rank_select(score, block_id, n_sel, ids):
    rank = jnp.zeros(score.shape, F32)
    for m, bid in enumerate(ids):
        cm = score[:, m:m + 1]
        beats = (cm > score) | ((cm == score) & (bid < block_id))
        rank = rank + jnp.where(beats, 1.0, 0.0)
    return jnp.where(rank < n_sel, 1.0, 0.0)


def _nsa_cmp_prompt_kernel(slope_ref, q_ref, kc_ref, vc_ref, o_ref, sel_ref, q_scr, *, n_rep, nb, n_sel, scale):
    t = QT
    d = kc_ref.shape[-1]
    nbp = kc_ref.shape[0]
    g = pl.program_id(1)
    i = pl.program_id(2)
    for r in range(n_rep):
        q_scr[r * t:(r + 1) * t, :] = q_ref[:, r * d:(r + 1) * d].astype(BF16)
    s_all = lax.dot_general(q_scr[...], kc_ref[...], (((1,), (1,)), ((), ())), preferred_element_type=F32)
    vc = vc_ref[...]
    pos = i * t + lax.broadcasted_iota(jnp.int32, (t, nbp), 0)
    blk = lax.broadcasted_iota(jnp.int32, (t, nbp), 1)
    real = blk < nb
    dist = pos - (blk * NSA_BLK + NSA_BLK - 1)
    distf = jnp.where(real, dist, 0).astype(F32)
    valid = (dist >= 0) & real
    score = jnp.zeros((t, nbp), F32)
    for r in range(n_rep):
        s = s_all[r * t:(r + 1) * t] * scale - slope_ref[g * n_rep + r] * distf
        s = jnp.where(valid, s, NEG_INF)
        m = jnp.max(s, axis=-1, keepdims=True)
        e = jnp.where(valid, jnp.exp(s - m), 0.0)
        p = e / jnp.maximum(jnp.sum(e, axis=-1, keepdims=True), TINY)
        score = score + p
        o_ref[:, r * d:(r + 1) * d] = jnp.dot(p.astype(BF16), vc, preferred_element_type=F32)
    cur = pos // NSA_BLK
    forced = (blk == 0) | (blk == cur) | (blk == cur - 1)
    causal = blk * NSA_BLK <= pos
    score = jnp.where(forced, NSA_FORCED, score)
    score = jnp.where(causal, score, -1.0)
    score = jnp.where(real, score, -2.0)
    sel_ref[...] = _rank_select(score, blk, n_sel, range(nb))


def nsa_cmp_prompt(proj, kc_cmp, vc_cmp, slopes, *, n_grp, n_rep, d):
    b, tlen, _ = proj.shape
    nb = kc_cmp.shape[2]
    nbp = _round_up(nb, LANE)
    kc_cmp = jnp.pad(kc_cmp, ((0, 0), (0, 0), (0, nbp - nb), (0, 0)))
    vc_cmp = jnp.pad(vc_cmp, ((0, 0), (0, 0), (0, nbp - nb), (0, 0)))
    t = QT
    return pl.pallas_call(
        functools.partial(_nsa_cmp_prompt_kernel, n_rep=n_rep, nb=nb, n_sel=min(NSA_TOPK, nb), scale=d ** -0.5),
        grid=(b, n_grp, tlen // t),
        in_specs=[pl.BlockSpec(memory_space=pltpu.SMEM),
                  pl.BlockSpec((None, t, n_rep * d), lambda bb, gg, i: (bb, i, gg)),
                  pl.BlockSpec((None, None, nbp, d), lambda bb, gg, i: (bb, gg, 0, 0)),
                  pl.BlockSpec((None, None, nbp, d), lambda bb, gg, i: (bb, gg, 0, 0))],
        out_specs=[pl.BlockSpec((None, t, n_rep * d), lambda bb, gg, i: (bb, i, gg)),
                   pl.BlockSpec((None, None, t, nbp), lambda bb, gg, i: (bb, gg, i, 0))],
        out_shape=[jax.ShapeDtypeStruct((b, tlen, n_grp * n_rep * d), F32),
                   jax.ShapeDtypeStruct((b, n_grp, tlen, nbp), F32)],
        scratch_shapes=[pltpu.VMEM((n_rep * t, d), BF16)],
        compiler_params=_cparams(("parallel", "parallel", "parallel"), 48),
        name="nsa_cmp_prompt",
    )(slopes.reshape(-1).astype(F32), proj, kc_cmp, vc_cmp)


def _nsa_combine_kernel(oc_ref, os_ref, ow_ref, gl_ref, bg_ref, o_ref, *, n_heads, d):
    gates = 1.0 / (1.0 + jnp.exp(-(gl_ref[...] + bg_ref[...])))
    for h in range(n_heads):
        sl = slice(h * d, (h + 1) * d)
        o = (oc_ref[:, sl] * gates[:, 3 * h:3 * h + 1] + os_ref[:, sl] * gates[:, 3 * h + 1:3 * h + 2]
             + ow_ref[:, sl] * gates[:, 3 * h + 2:3 * h + 3])
        o_ref[:, sl] = o.astype(o_ref.dtype)


def nsa_combine(o_cmp, o_sel, o_win, proj, b_gate, gate_col):
    m, n = o_cmp.shape
    tm = _pick(m, (256, 128, 64, 32, 16, 8))
    blk = lambda: pl.BlockSpec((tm, n), lambda i: (i, 0))
    bg = jnp.pad(b_gate.astype(F32), (0, LANE - b_gate.shape[0])).reshape(1, LANE)
    return pl.pallas_call(
        functools.partial(_nsa_combine_kernel, n_heads=B_HEADS, d=B_HD),
        grid=(m // tm,),
        in_specs=[blk(), blk(), blk(),
                  pl.BlockSpec((tm, LANE), lambda i: (i, gate_col // LANE)),
                  pl.BlockSpec((1, LANE), lambda i: (0, 0))],
        out_specs=blk(),
        out_shape=jax.ShapeDtypeStruct((m, n), BF16),
        compiler_params=_cparams(("parallel",), 48),
        name="nsa_combine",
    )(o_cmp, o_sel, o_win, proj, bg)


def _decode_kernel(*refs, n_grp, n_maps, n_rep, dv, n_pages, n_steps, paged, chunk, kpos0, pos, window,
                   scale, has_mask, has_sink, diff, lam_init):
    it = iter(refs)
    if paged:
        next(it)
    q_ref = next(it)
    k_refs = [next(it) for _ in range(n_pages)]
    v_refs = [next(it) for _ in range(n_pages)]
    kn_ref, vn_ref, slope_ref = next(it), next(it), next(it)
    sink_ref = next(it) if has_sink else None
    mask_ref, maskn_ref = (next(it), next(it)) if has_mask else (None, None)
    lam_ref, subln_ref = (next(it), next(it)) if diff else (None, None)
    o_ref = next(it)
    m_ref, l_ref, acc_ref = next(it), next(it), next(it)

    rows = n_maps * n_rep
    hk = n_grp * n_maps
    vh = max(dv // LANE, 1)
    hv = vh * n_grp
    npos = chunk // n_pages
    step = pl.program_id(1)

    @pl.when(step == 0)
    def _():
        if has_sink:
            m_ref[...] = sink_ref[...]
            l_ref[...] = jnp.ones(l_ref.shape, F32)
        else:
            m_ref[...] = jnp.full(m_ref.shape, NEG_INF, F32)
            l_ref[...] = jnp.zeros(l_ref.shape, F32)
        acc_ref[...] = jnp.zeros(acc_ref.shape, F32)

    kpos = kpos0 + step * chunk + lax.broadcasted_iota(jnp.int32, (1, chunk), 1)
    dist = pos - kpos
    ok = (dist >= 0) & (kpos >= 0)
    if window is not None:
        ok = ok & (dist <= window)
    distf = dist.astype(F32)
    for g in range(n_grp):
        okf = jnp.where(ok, 1.0, 0.0)
        if has_mask:
            okf = okf * mask_ref[g]
        okg = jnp.broadcast_to(okf, (rows, chunk)) > 0.5
        kcat = jnp.concatenate(
            [jnp.concatenate([r[pl.ds(g * n_maps + c, npos, stride=hk), :] for c in range(n_maps)], axis=1)
             for r in k_refs], axis=0).astype(BF16)
        vcat = jnp.concatenate(
            [jnp.concatenate([r[pl.ds(h * n_grp + g, npos, stride=hv), :] for h in range(vh)], axis=1)
             for r in v_refs], axis=0).astype(BF16)
        s = lax.dot_general(q_ref[g], kcat, (((1,), (1,)), ((), ())), preferred_element_type=F32)
        s = s * scale - slope_ref[g] * distf
        s = jnp.where(okg, s, NEG_INF)
        m_old = m_ref[g]
        m_new = jnp.maximum(m_old, jnp.max(s, axis=-1, keepdims=True))
        alpha = jnp.exp(m_old - m_new)
        e = jnp.where(okg, jnp.exp(s - m_new), 0.0)
        l_ref[g] = alpha * l_ref[g] + jnp.sum(e, axis=-1, keepdims=True)
        acc_ref[g] = alpha * acc_ref[g] + jnp.dot(e.astype(BF16), vcat, preferred_element_type=F32)
        m_ref[g] = m_new

    @pl.when(step == n_steps - 1)
    def _():
        for g in range(n_grp):
            kn = kn_ref[g].astype(BF16).astype(F32)
            vn = vn_ref[g].astype(BF16).astype(F32)
            s = jnp.sum(q_ref[g].astype(F32) * kn, axis=-1, keepdims=True) * scale
            if has_mask:
                okn = maskn_ref[g][:, 0:1] > 0.5
                s = jnp.where(okn, s, NEG_INF)
            m_old = m_ref[g]
            m_new = jnp.maximum(m_old, s)
            alpha = jnp.exp(m_old - m_new)
            e = jnp.exp(s - m_new)
            if has_mask:
                e = jnp.where(okn, e, 0.0)
            den = alpha * l_ref[g] + e
            o = (alpha * acc_ref[g] + e * vn) / jnp.maximum(den, TINY)
            if diff:
                o = _diff_finish(o[0:n_rep], o[n_rep:rows], lam_ref, subln_ref, lam_init)
            o_ref[g] = o.astype(o_ref.dtype)


def _block_diag_q(q):
    b, g, c, r, d = q.shape
    eye = jnp.eye(c, dtype=q.dtype)
    return jnp.einsum('bgcrd,ce->bgcred', q, eye).reshape(b, g, c * r, c * d)


def decode_attn(q, k_main, v_main, k_new, v_new, slopes, *, pos, page_table=None, pages_per_step=1,
                window=None, sink=None, mask=None, diff=None, out_dtype=F32):
    b, n_grp, n_maps, n_rep, d = q.shape
    dv = v_new.shape[-1]
    rows = n_maps * n_rep
    hk = n_grp * n_maps
    paged = page_table is not None
    if paged:
        n_pages = pages_per_step
        n_steps = page_table.shape[1] // n_pages
        chunk = n_pages * PAGE
        kpos0 = 0
        total = page_table.shape[1] * PAGE
    else:
        n_pages, n_steps = 1, 1
        chunk = k_main.shape[1] // hk
        kpos0 = pos - chunk
        total = chunk

    def idx(f):
        if paged:
            return lambda bb, st, pt: f(bb, st, pt)
        return lambda bb, st: f(bb, st, None)

    in_specs = [pl.BlockSpec((None, n_grp, rows, n_maps * d), idx(lambda bb, st, pt: (bb, 0, 0, 0)))]
    args = [_block_diag_q(q)]
    for main in (k_main, v_main):
        blk = (None,) + main.shape[1:]
        for p in range(n_pages):
            if paged:
                in_specs.append(pl.BlockSpec(blk, lambda bb, st, pt, p=p: (pt[bb, st * n_pages + p], 0, 0)))
            else:
                in_specs.append(pl.BlockSpec(blk, lambda bb, st: (bb, 0, 0)))
            args.append(main)
    col = lambda x: jnp.tile(x.astype(F32).reshape(n_grp, 1, n_rep), (1, n_maps, 1)).reshape(n_grp, rows, 1)
    in_specs += [pl.BlockSpec((None, n_grp, 1, n_maps * d), idx(lambda bb, st, pt: (bb, 0, 0, 0))),
                 pl.BlockSpec((None, n_grp, 1, dv), idx(lambda bb, st, pt: (bb, 0, 0, 0))),
                 pl.BlockSpec((n_grp, rows, 1), idx(lambda bb, st, pt: (0, 0, 0)))]
    args += [k_new, v_new, col(slopes)]
    if sink is not None:
        in_specs.append(pl.BlockSpec((n_grp, rows, 1), idx(lambda bb, st, pt: (0, 0, 0))))
        args.append(col(sink))
    if mask is not None:
        in_specs.append(pl.BlockSpec((None, n_grp, 1, chunk), idx(lambda bb, st, pt: (bb, 0, 0, st))))
        in_specs.append(pl.BlockSpec((None, n_grp, 1, LANE), idx(lambda bb, st, pt: (bb, 0, 0, total // LANE))))
        args += [mask, mask]
    lam_init = 0.0
    if diff is not None:
        lam_vecs, subln, lam_init = diff
        in_specs.append(pl.BlockSpec((4, A_HD), idx(lambda bb, st, pt: (0, 0))))
        in_specs.append(pl.BlockSpec((1, dv), idx(lambda bb, st, pt: (0, 0))))
        args += [lam_vecs.astype(F32), subln.reshape(1, dv).astype(F32)]
    out_spec = pl.BlockSpec((None, n_grp, n_rep, dv), idx(lambda bb, st, pt: (bb, 0, 0, 0)))
    scratch = [pltpu.VMEM((n_grp, rows, 1), F32), pltpu.VMEM((n_grp, rows, 1), F32),
               pltpu.VMEM((n_grp, rows, dv), F32)]
    body = functools.partial(
        _decode_kernel, n_grp=n_grp, n_maps=n_maps, n_rep=n_rep, dv=dv, n_pages=n_pages, n_steps=n_steps,
        paged=paged, chunk=chunk, kpos0=kpos0, pos=pos, window=window, scale=d ** -0.5,
        has_mask=mask is not None, has_sink=sink is not None, diff=diff is not None, lam_init=lam_init)
    out_shape = jax.ShapeDtypeStruct((b, n_grp, n_rep, dv), out_dtype)
    cp = _cparams(("parallel", "arbitrary"), 48)
    if paged:
        return pl.pallas_call(
            body,
            grid_spec=pltpu.PrefetchScalarGridSpec(num_scalar_prefetch=1, grid=(b, n_steps), in_specs=in_specs,
                                                   out_specs=out_spec, scratch_shapes=scratch),
            out_shape=out_shape, compiler_params=cp, name="decode_paged",
        )(page_table, *args)
    return pl.pallas_call(body, grid=(b, n_steps), in_specs=in_specs, out_specs=out_spec, out_shape=out_shape,
                          scratch_shapes=scratch, compiler_params=cp, name="decode_dense")(*args)


def _pool_compress_kernel(x_ref, w_ref, o_ref, *, n_pages, n_grp, d):
    halves = PAGE // NSA_BLK
    page_rows = PAGE * n_grp
    acc = jnp.zeros((halves * n_grp * n_pages, d), F32)
    for j in range(NSA_BLK):
        a = jnp.concatenate([x_ref[pl.ds((h * NSA_BLK + j) * n_grp + g, n_pages, stride=page_rows), :]
                             for h in range(halves) for g in range(n_grp)], axis=0).astype(BF16)
        acc = acc + jnp.dot(a, w_ref[j], preferred_element_type=F32)
    for hg in range(halves * n_grp):
        o_ref[:, hg * d:(hg + 1) * d] = acc[hg * n_pages:(hg + 1) * n_pages]


def pool_compress(cache, w_cmp):
    npool, _, n_grp, d = cache.shape
    n_pages = _pick(npool, (64, 32, 16, 8))
    halves = PAGE // NSA_BLK
    return pl.pallas_call(
        functools.partial(_pool_compress_kernel, n_pages=n_pages, n_grp=n_grp, d=d),
        grid=(npool // n_pages,),
        in_specs=[pl.BlockSpec((n_pages * PAGE * n_grp, d), lambda i: (i, 0)),
                  pl.BlockSpec((NSA_BLK, d, d), lambda i: (0, 0, 0))],
        out_specs=pl.BlockSpec((n_pages, halves * n_grp * d), lambda i: (i, 0)),
        out_shape=jax.ShapeDtypeStruct((npool, halves * n_grp * d), F32),
        compiler_params=_cparams(("parallel",), 40),
        name="pool_compress",
    )(cache.reshape(npool * PAGE * n_grp, d), w_cmp.astype(BF16))


def _nsa_cmp_sample_kernel(pt_ref, q_ref, kpool_ref, vpool_ref, kn_ref, vn_ref, wk0_ref, wv0_ref, slope_ref,
                           e_ref, o_ref, mask_ref, kc_scr, vc_scr, *, n_grp, n_rep, n_pg, nbp, n_sel, pos,
                           scale, ids):
    b = pl.program_id(0)
    d = q_ref.shape[-1]
    gw = n_grp * d
    for p in range(n_pg):
        pid = pt_ref[b, p]
        kc_scr[pl.ds(p, 1), :] = kpool_ref[pl.ds(pid, 1), :]
        vc_scr[pl.ds(p, 1), :] = vpool_ref[pl.ds(pid, 1), :]
    nb = 2 * n_pg + 1
    col = lax.broadcasted_iota(jnp.int32, (1, nbp), 1)
    nid = jnp.where(col < n_pg, 2 * col, jnp.where(col < 2 * n_pg, 2 * (col - n_pg) + 1,
                                                   jnp.where(col == 2 * n_pg, 2 * n_pg, 1 << 20)))
    real = col < nb
    dist = pos - (nid * NSA_BLK + NSA_BLK - 1)
    valid = (dist >= 0) & real
    distf = jnp.where(real, dist, 0).astype(F32)
    cur = pos // NSA_BLK
    forced = (nid == 0) | (nid == cur) | (nid == cur - 1)
    causal = nid * NSA_BLK <= pos
    kn_cmp = jnp.dot(kn_ref[...].astype(BF16), wk0_ref[...], preferred_element_type=F32)
    vn_cmp = jnp.dot(vn_ref[...].astype(BF16), wv0_ref[...], preferred_element_type=F32)
    tail = nbp - 2 * n_pg
    first_row = lax.broadcasted_iota(jnp.int32, (tail, d), 0) == 0
    sels = []
    for g in range(n_grp):
        def blocks(scr, new):
            new_blk = jnp.where(first_row, jnp.broadcast_to(new[g:g + 1], (tail, d)), 0.0)
            return jnp.concatenate([scr[:, g * d:(g + 1) * d], scr[:, gw + g * d:gw + (g + 1) * d],
                                    new_blk], axis=0).astype(BF16)
        kc = blocks(kc_scr, kn_cmp)
        vc = blocks(vc_scr, vn_cmp)
        s = lax.dot_general(q_ref[g], kc, (((1,), (1,)), ((), ())), preferred_element_type=F32)
        s = s * scale - slope_ref[g] * distf
        s = jnp.where(valid, s, NEG_INF)
        m = jnp.max(s, axis=-1, keepdims=True)
        e = jnp.where(valid, jnp.exp(s - m), 0.0)
        p = e / jnp.maximum(jnp.sum(e, axis=-1, keepdims=True), TINY)
        o_ref[g] = jnp.dot(p.astype(BF16), vc, preferred_element_type=F32)
        score = jnp.sum(p, axis=0, keepdims=True)
        score = jnp.where(forced, NSA_FORCED, score)
        score = jnp.where(causal, score, -1.0)
        score = jnp.where(real, score, -2.0)
        sels.append(_rank_select(score, nid, n_sel, ids))
    sel = jnp.concatenate(sels + [jnp.zeros((8 - n_grp, nbp), F32)], axis=0)
    km = jnp.dot(sel.astype(BF16), e_ref[...], preferred_element_type=F32)
    for g in range(n_grp):
        mask_ref[g] = km[g:g + 1]


def nsa_cmp_sample(q, kpool, vpool, kc_new, vc_new, wk0, wv0, slopes, page_table, pos):
    b, n_grp, n_rep, d = q.shape
    n_pg = page_table.shape[1]
    nb = 2 * n_pg + 1
    nbp = _round_up(nb, LANE)
    total = n_pg * PAGE
    ids = [2 * s for s in range(n_pg)] + [2 * s + 1 for s in range(n_pg)] + [2 * n_pg]
    slot_of_block = {bid: s for s, bid in enumerate(ids)}
    key_slot = jnp.asarray([slot_of_block[kp // NSA_BLK] for kp in range(total)]
                           + [slot_of_block[2 * n_pg]] + [-1] * (LANE - 1), jnp.int32)
    expand = (jnp.arange(nbp, dtype=jnp.int32)[:, None] == key_slot[None, :]).astype(BF16)
    npool = kpool.shape[0]
    full = lambda shape: pl.BlockSpec(shape, lambda bb, pt: tuple(0 for _ in shape))
    return pl.pallas_call(
        functools.partial(_nsa_cmp_sample_kernel, n_grp=n_grp, n_rep=n_rep, n_pg=n_pg, nbp=nbp,
                          n_sel=min(NSA_TOPK, nb), pos=pos, scale=d ** -0.5, ids=ids),
        grid_spec=pltpu.PrefetchScalarGridSpec(
            num_scalar_prefetch=1, grid=(b,),
            in_specs=[pl.BlockSpec((None, n_grp, n_rep, d), lambda bb, pt: (bb, 0, 0, 0)),
                      full((npool, 2 * n_grp * d)), full((npool, 2 * n_grp * d)),
                      pl.BlockSpec((None, 8, d), lambda bb, pt: (bb, 0, 0)),
                      pl.BlockSpec((None, 8, d), lambda bb, pt: (bb, 0, 0)),
                      full((d, d)), full((d, d)), full((n_grp, n_rep, 1)), full((nbp, total + LANE))],
            out_specs=[pl.BlockSpec((None, n_grp, n_rep, d), lambda bb, pt: (bb, 0, 0, 0)),
                       pl.BlockSpec((None, n_grp, 1, total + LANE), lambda bb, pt: (bb, 0, 0, 0))],
            scratch_shapes=[pltpu.VMEM((n_pg, 2 * n_grp * d), F32), pltpu.VMEM((n_pg, 2 * n_grp * d), F32)]),
        out_shape=[jax.ShapeDtypeStruct((b, n_grp, n_rep, d), F32),
                   jax.ShapeDtypeStruct((b, n_grp, 1, total + LANE), F32)],
        compiler_params=_cparams(("arbitrary",), 56),
        name="nsa_cmp_sample",
    )(page_table, q, kpool, vpool, kc_new, vc_new, wk0, wv0, slopes.astype(F32).reshape(n_grp, n_rep, 1), expand)


def diff_layer(xp, xs, hp, hs, bsz, tlen, past, cache_k, cache_v, page_table, w_in, lam_vecs, subln, w_o,
               layer_idx):
    lam_init = 0.8 - 0.6 * math.exp(-0.3 * layer_idx)
    slopes = alibi_slopes(A_HEADS, A_KV)
    n_rep = A_HEADS // A_KV
    a_q, a_kw = A_HEADS * 2 * A_HD, A_KV * 2 * A_HD
    w_in_b, w_o_b = w_in, w_o
    diff = (lam_vecs, subln, lam_init)

    proj = matmul(hp, w_in_b).reshape(bsz, tlen, a_q + 2 * a_kw)
    kp = proj[..., a_q:a_q + a_kw].reshape(bsz, tlen, A_KV, 2, A_HD)
    vp = proj[..., a_q + a_kw:].reshape(bsz, tlen, A_KV, 2 * A_HD)
    o = causal_prompt(proj, slopes, n_grp=A_KV, n_rep=n_rep, n_maps=2, d=A_HD, dv=2 * A_HD, q_col=0, k_col=a_q,
                      v_col=a_q + a_kw, diff=diff, out_dtype=BF16)
    xp = matmul(o.reshape(bsz * tlen, a_q), w_o_b, res=xp)

    nb = hs.shape[0]
    npool = cache_k.shape[0]
    proj_s = matmul(hs, w_in_b)
    qs = proj_s[:, :a_q].reshape(nb, A_KV, n_rep, 2, A_HD).transpose(0, 1, 3, 2, 4).astype(BF16)
    ks = proj_s[:, a_q:a_q + a_kw].reshape(nb, 1, A_KV, 2, A_HD)
    vs = proj_s[:, a_q + a_kw:].reshape(nb, 1, A_KV, 2 * A_HD)
    v_rows = cache_v.reshape(npool, PAGE, A_KV, 2, A_HD).transpose(0, 1, 3, 2, 4).reshape(npool, PAGE * 2 * A_KV, A_HD)
    os_ = decode_attn(qs, cache_k.reshape(npool, PAGE * A_KV * 2, A_HD), v_rows,
                      ks.reshape(nb, A_KV, 1, 2 * A_HD), vs.reshape(nb, A_KV, 1, 2 * A_HD),
                      slopes, pos=past, page_table=page_table, pages_per_step=8, diff=diff, out_dtype=BF16)
    xs = matmul(os_.reshape(nb, a_q), w_o_b, res=xs)
    return xp, xs, (kp, ks, vp, vs)


def _cmp_weight(w_cmp):
    blk, d, _ = w_cmp.shape
    eye = jnp.eye(B_KV, dtype=w_cmp.dtype)
    return jnp.einsum('jde,gh->jgdhe', w_cmp, eye).reshape(blk * B_KV * d, B_KV * d).astype(BF16)


def nsa_layer(xp, xs, hp, hs, bsz, tlen, past, cache_kc, cache_vc, cache_ks, cache_vs, buf_kw, buf_vw,
              page_table, w_in, b_gate, w_cmp_k, w_cmp_v, w_o):
    slopes = alibi_slopes(B_HEADS, B_KV)
    n_rep = B_HEADS // B_KV
    b_q, b_kw = B_HEADS * B_HD, B_KV * B_HD
    n_in = w_in.shape[1]
    n_pad = _round_up(n_in, 1024)
    gate_col = b_q + 6 * b_kw
    w_in_b = jnp.pad(w_in, ((0, 0), (0, n_pad - n_in)))
    w_o_b = w_o
    wck, wcv = _cmp_weight(w_cmp_k), _cmp_weight(w_cmp_v)
    kv_col = lambda i: b_q + i * b_kw

    proj = matmul(hp, w_in_b)
    proj3 = proj.reshape(bsz, tlen, n_pad)
    kv = [proj3[..., kv_col(i):kv_col(i + 1)].reshape(bsz, tlen, B_KV, B_HD) for i in range(6)]
    nblk = tlen // NSA_BLK
    cmp_in = lambda a: a.reshape(bsz * nblk, NSA_BLK * b_kw).astype(BF16)
    kc_cmp = matmul(cmp_in(kv[0]), wck).reshape(bsz, nblk, B_KV, B_HD).transpose(0, 2, 1, 3).astype(BF16)
    vc_cmp = matmul(cmp_in(kv[1]), wcv).reshape(bsz, nblk, B_KV, B_HD).transpose(0, 2, 1, 3).astype(BF16)
    o_cmp, sel = nsa_cmp_prompt(proj3, kc_cmp, vc_cmp, slopes, n_grp=B_KV, n_rep=n_rep, d=B_HD)
    expand = (jnp.arange(_round_up(nblk, LANE), dtype=jnp.int32)[:, None]
              == (jnp.arange(tlen, dtype=jnp.int32) // NSA_BLK)[None, :]).astype(BF16)
    o_sel = causal_prompt(proj3, slopes, n_grp=B_KV, n_rep=n_rep, n_maps=1, d=B_HD, dv=B_HD, q_col=0,
                          k_col=kv_col(2), v_col=kv_col(3), sel=(sel, expand))
    o_win = banded_prompt(proj3, proj3, proj3, slopes, window=NSA_WIN, tq=NSA_WIN, n_grp=B_KV, n_rep=n_rep,
                          d=B_HD, dv=B_HD, q_col=0, k_col=kv_col(4), v_col=kv_col(5), rep_chunk=4)
    m = bsz * tlen
    o = nsa_combine(o_cmp.reshape(m, b_q), o_sel.reshape(m, b_q), o_win.reshape(m, b_q), proj, b_gate, gate_col)
    xp = matmul(o, w_o_b, res=xp)

    nb = hs.shape[0]
    proj_s = matmul(hs, w_in_b)
    qs = proj_s[:, :b_q].reshape(nb, B_KV, n_rep, B_HD).astype(BF16)
    kvs = [proj_s[:, kv_col(i):kv_col(i + 1)].reshape(nb, 1, B_KV, B_HD) for i in range(6)]
    kpool = pool_compress(cache_kc, w_cmp_k)
    vpool = pool_compress(cache_vc, w_cmp_v)
    new8 = lambda a: jnp.pad(a.reshape(nb, B_KV, B_HD), ((0, 0), (0, 8 - B_KV), (0, 0)))
    o_cmp_s, keymask = nsa_cmp_sample(qs, kpool, vpool, new8(kvs[0]), new8(kvs[1]),
                                      w_cmp_k[0].astype(BF16), w_cmp_v[0].astype(BF16), slopes, page_table, past)
    qs5 = qs[:, :, None]
    new_row = lambda a: a.reshape(nb, B_KV, 1, B_HD)
    head_rows = lambda c: c.reshape(c.shape[0], c.shape[1] * B_KV, B_HD)
    o_sel_s = decode_attn(qs5, head_rows(cache_ks), head_rows(cache_vs), new_row(kvs[2]), new_row(kvs[3]), slopes,
                          pos=past, page_table=page_table, pages_per_step=8, mask=keymask)
    o_win_s = decode_attn(qs5, head_rows(buf_kw), head_rows(buf_vw), new_row(kvs[4]), new_row(kvs[5]), slopes,
                          pos=past, window=NSA_WIN)
    os_ = nsa_combine(o_cmp_s.reshape(nb, b_q), o_sel_s.reshape(nb, b_q), o_win_s.reshape(nb, b_q), proj_s,
                      b_gate, gate_col)
    xs = matmul(os_, w_o_b, res=xs)

    wb = buf_kw.shape[1]
    wp = min(NSA_WIN, tlen)
    kw_all = jnp.concatenate([buf_kw, kvs[4]], axis=1)
    vw_all = jnp.concatenate([buf_vw, kvs[5]], axis=1)
    state = (kv[0], kvs[0], kv[1], kvs[1], kv[2], kvs[2], kv[3], kvs[3],
             kv[4][:, tlen - wp:], kw_all[:, -wb:], kv[5][:, tlen - wp:], vw_all[:, -wb:])
    return xp, xs, state


def swa_layer(xp, xs, hp, hs, bsz, tlen, past, buf_k, buf_v, w_in, b_in, sinks, w_o, b_o):
    slopes = alibi_slopes(C_HEADS, C_KV)
    sink = sinks.astype(F32).reshape(C_KV, C_HEADS // C_KV)
    n_rep = C_HEADS // C_KV
    c_q, c_kw = C_HEADS * C_HD, C_KV * C_HD
    w_in_b, w_o_b = w_in, w_o
    heads = lambda a: a.transpose(0, 2, 1, 3).astype(BF16)

    proj = matmul(hp, w_in_b, bias=b_in).reshape(bsz, tlen, c_q + 2 * c_kw)
    q = proj[..., :c_q].reshape(bsz, tlen, C_KV, n_rep, C_HD).transpose(0, 2, 3, 1, 4).astype(BF16)
    kp = proj[..., c_q:c_q + c_kw].reshape(bsz, tlen, C_KV, C_HD)
    vp = proj[..., c_q + c_kw:].reshape(bsz, tlen, C_KV, C_HD)
    o = banded_prompt(q, heads(kp), heads(vp), slopes, window=C_WIN, tq=C_WIN, n_grp=C_KV, n_rep=n_rep, d=C_HD,
                      dv=C_HD, sink=sink, out_dtype=BF16)
    o = o.transpose(0, 3, 1, 2, 4).reshape(bsz * tlen, c_q)
    xp = matmul(o, w_o_b, bias=b_o, res=xp)

    nb = hs.shape[0]
    proj_s = matmul(hs, w_in_b, bias=b_in)
    qs = proj_s[:, :c_q].reshape(nb, C_KV, 1, n_rep, C_HD).astype(BF16)
    ks = proj_s[:, c_q:c_q + c_kw].reshape(nb, 1, C_KV, C_HD)
    vs = proj_s[:, c_q + c_kw:].reshape(nb, 1, C_KV, C_HD)
    head_rows = lambda c: c.reshape(c.shape[0], c.shape[1] * C_KV, C_HD)
    os_ = decode_attn(qs, head_rows(buf_k), head_rows(buf_v), ks.reshape(nb, C_KV, 1, C_HD),
                      vs.reshape(nb, C_KV, 1, C_HD), slopes, pos=past, window=C_WIN, sink=sink, out_dtype=BF16)
    xs = matmul(os_.reshape(nb, c_q), w_o_b, bias=b_o, res=xs)

    wb = buf_k.shape[1]
    wp = min(C_WIN, tlen)
    k_all = jnp.concatenate([buf_k, ks], axis=1)
    v_all = jnp.concatenate([buf_v, vs], axis=1)
    return xp, xs, (kp[:, tlen - wp:], k_all[:, -wb:], vp[:, tlen - wp:], v_all[:, -wb:])


def kernel(x_prompt, x_sample, cache_l0_k, cache_l0_v, cache_l1_kc, cache_l1_vc, cache_l1_ks, cache_l1_vs, state_l1_kw, state_l1_vw, state_l2_k, state_l2_v, cache_l3_k, cache_l3_v, page_table, norm_ffa, norm_mix, norm_ffb, norm_final, ffa_w13, ffa_w2, ffb_w13, ffb_w2, l0_w_in, l0_lam, l0_subln, l0_w_o, l1_w_in, l1_b_gate, l1_w_cmp_k, l1_w_cmp_v, l1_w_o, l2_w_in, l2_b_in, l2_sinks, l2_w_o, l2_b_o, l3_w_in, l3_lam, l3_subln, l3_w_o):
    bsz, tlen, dm = x_prompt.shape
    nb, dec_seq, _ = x_sample.shape
    assert dec_seq == 1 and dm == D_MODEL
    past = page_table.shape[1] * PAGE
    xp = x_prompt.reshape(bsz * tlen, dm)
    xs = x_sample.reshape(nb, dm)
    ffa13, ffb13 = ffa_w13, ffb_w13
    ffa2, ffb2 = ffa_w2.astype(BF16), ffb_w2.astype(BF16)
    a_params = {0: (cache_l0_k, cache_l0_v, l0_w_in, l0_lam, l0_subln, l0_w_o),
                3: (cache_l3_k, cache_l3_v, l3_w_in, l3_lam, l3_subln, l3_w_o)}
    state = {}
    for i in range(4):
        xp, xs = ffn_block(xp, xs, norm_ffa[i], ffa13, ffa2, i)
        hp = rmsnorm(xp, norm_mix[i], BF16)
        hs = rmsnorm(xs, norm_mix[i], BF16)
        kind = i % 3
        if kind == 0:
            ck, cv, w_in, lam, subln, w_o = a_params[i]
            xp, xs, st = diff_layer(xp, xs, hp, hs, bsz, tlen, past, ck, cv, page_table, w_in, lam, subln, w_o, i)
        elif kind == 1:
            xp, xs, st = nsa_layer(xp, xs, hp, hs, bsz, tlen, past, cache_l1_kc, cache_l1_vc, cache_l1_ks,
                                   cache_l1_vs, state_l1_kw, state_l1_vw, page_table, l1_w_in, l1_b_gate,
                                   l1_w_cmp_k, l1_w_cmp_v, l1_w_o)
        else:
            xp, xs, st = swa_layer(xp, xs, hp, hs, bsz, tlen, past, state_l2_k, state_l2_v, l2_w_in, l2_b_in,
                                   l2_sinks, l2_w_o, l2_b_o)
        state[i] = st
        xp, xs = ffn_block(xp, xs, norm_ffb[i], ffb13, ffb2, i)
    y_prompt = rmsnorm(xp, norm_final, F32).reshape(bsz, tlen, dm)
    y_sample = rmsnorm(xs, norm_final, F32).reshape(nb, 1, dm)
    return (y_prompt, y_sample) + tuple(state[0]) + tuple(state[1]) + tuple(state[2]) + tuple(state[3])
```
